```python
import math
import jax, jax.numpy as jnp
from jax import lax
import numpy as np

D_MODEL = 2048
BATCH = 2
SEQ = 4096
DEPTH = 2

GRID_W = 64
CTX_LEN = 256
HEAD_DIM = 128
A_Q_HEADS = 6
A_KV_HEADS = 2
B_HEADS = 4
B_QK_DIM = 64
B_V_DIM = 2 * B_QK_DIM
C_Q_HEADS = 6
C_KV_HEADS = 2
WINDOW = 128
QBLK = 128
N_EXPERTS = 16
EXPERT_FF = 1024
EC_CAPACITY = 2
ROPE_THETA = 10000.0
EPS = 1e-6
NEG_INF = -1e30

A_Q_W = A_Q_HEADS * HEAD_DIM
A_KV_W = A_KV_HEADS * HEAD_DIM
B_QK_W = B_HEADS * 2 * B_QK_DIM
B_V_W = B_HEADS * B_V_DIM
C_Q_W = C_Q_HEADS * HEAD_DIM
C_KV_W = C_KV_HEADS * HEAD_DIM
KV_W = 2 * A_KV_W + B_QK_W + B_V_W + 2 * C_KV_W
Q_W = A_Q_W + B_QK_W + C_Q_W
GATE_W = 3 * D_MODEL
IN_W = KV_W + Q_W + GATE_W
KV_SPLITS = (A_KV_W, 2 * A_KV_W, 2 * A_KV_W + B_QK_W, 2 * A_KV_W + B_QK_W + B_V_W,
             2 * A_KV_W + B_QK_W + B_V_W + C_KV_W)
Q_SPLITS = (A_Q_W, A_Q_W + B_QK_W)

kernel_name = "hybrid_parallel_gqa_diff_window_ec_moe"


def rms_norm(x, g):
    xf = x.astype(jnp.float32)
    y = xf * lax.rsqrt(jnp.mean(xf * xf, axis=-1, keepdims=True) + EPS)
    return y.astype(x.dtype) * g


def modulate(h, shift, scale):
    return h * (1 + scale) + shift


def axial_rope_tables(row, col, dh, dtype):
    d_ax = dh // 2
    inv = ROPE_THETA ** (-jnp.arange(0, d_ax, 2, dtype=jnp.float32) / d_ax)
    fr = row.astype(jnp.float32)[:, None] * inv
    fc = col.astype(jnp.float32)[:, None] * inv
    ang = jnp.concatenate([fr, fr, fc, fc], axis=-1)
    return jnp.cos(ang)[:, None, :].astype(dtype), jnp.sin(ang)[:, None, :].astype(dtype)


def rotate_quarters(x):
    x1, x2, x3, x4 = jnp.split(x, 4, axis=-1)
    return jnp.concatenate([-x2, x1, -x4, x3], axis=-1)


def apply_rope(t, rope):
    cos, sin = rope
    return t * cos + rotate_quarters(t) * sin


def qk_heads(t, n, d, gain, rope):
    t = rms_norm(t.reshape(t.shape[0], t.shape[1], n, d), gain)
    if rope is not None:
        t = apply_rope(t, rope)
    return t.transpose(0, 2, 1, 3)


def v_heads(t, n, d):
    return t.reshape(t.shape[0], t.shape[1], n, d).transpose(0, 2, 1, 3)


def group_heads(t, n_kv):
    b, h, tt, d = t.shape
    return t.reshape(b, n_kv, h // n_kv, tt, d)


def split_pairs(t):
    b, h2, tt, d = t.shape
    t = t.reshape(b, h2 // 2, 2, tt, d)
    return t[:, :, 0], t[:, :, 1]


def heads_to_tokens(o):
    b, tt, d = o.shape[0], o.shape[-2], o.shape[-1]
    return o.reshape(b, -1, tt, d).transpose(0, 2, 1, 3).reshape(b, tt, -1)


def gqa_block(q, k, v, sink=None):
    s = jnp.einsum('bhgqd,bhkd->bhgqk', q, k).astype(jnp.float32) * (q.shape[-1] ** -0.5)
    if sink is not None:
        s_sink = jnp.broadcast_to(sink.astype(jnp.float32)[None, :, :, None, None], s.shape[:-1] + (1,))
        p = jax.nn.softmax(jnp.concatenate([s, s_sink], axis=-1), axis=-1)[..., :-1]
    else:
        p = jax.nn.softmax(s, axis=-1)
    return jnp.einsum('bhgqk,bhkd->bhgqd', p.astype(v.dtype), v)


def diff_block(q1, q2, k1, k2, v, lam):
    scale = q1.shape[-1] ** -0.5
    s1 = jnp.einsum('bhqd,bhkd->bhqk', q1, k1).astype(jnp.float32) * scale
    s2 = jnp.einsum('bhqd,bhkd->bhqk', q2, k2).astype(jnp.float32) * scale
    p = jax.nn.softmax(s1, axis=-1) - lam * jax.nn.softmax(s2, axis=-1)
    return jnp.einsum('bhqk,bhkd->bhqd', p.astype(v.dtype), v)


def sweep_blocks(fn, qs):
    nb = qs[0].shape[-2] // QBLK

    def to_blocks(q):
        return jnp.moveaxis(q.reshape(q.shape[:-2] + (nb, QBLK, q.shape[-1])), -3, 0)

    o = lax.map(lambda qb: fn(*qb), tuple(to_blocks(q) for q in qs))
    o = jnp.moveaxis(o, 0, -3)
    return o.reshape(o.shape[:-3] + (nb * QBLK, o.shape[-1]))


def window_attention(q, k, v, k_ctx, v_ctx, sink):
    b, hkv, g, s_len, d = q.shape
    nb = s_len // QBLK
    scale = d ** -0.5
    qb = q.reshape(b, hkv, g, nb, QBLK, d)

    def band(t):
        tp = jnp.pad(t, ((0, 0), (0, 0), (QBLK, QBLK), (0, 0))).reshape(b, hkv, nb + 2, QBLK, d)
        return jnp.concatenate([tp[:, :, :-2], tp[:, :, 1:-1], tp[:, :, 2:]], axis=3)

    kb, vb = band(k), band(v)
    s_loc = jnp.einsum('bhgnqd,bhnkd->bhgnqk', qb, kb).astype(jnp.float32) * scale
    blk = jnp.arange(nb)[:, None, None]
    qpos = blk * QBLK + jnp.arange(QBLK)[None, :, None]
    kpos = (blk - 1) * QBLK + jnp.arange(3 * QBLK)[None, None, :]
    valid = (jnp.abs(kpos - qpos) <= WINDOW) & (kpos >= 0) & (kpos < s_len)
    s_loc = jnp.where(valid, s_loc, NEG_INF)
    s_ctx = jnp.einsum('bhgnqd,bhkd->bhgnqk', qb, k_ctx).astype(jnp.float32) * scale
    s_sink = jnp.broadcast_to(sink.astype(jnp.float32)[None, :, :, None, None, None], s_loc.shape[:-1] + (1,))
    p = jax.nn.softmax(jnp.concatenate([s_loc, s_ctx, s_sink], axis=-1), axis=-1)
    n_loc, n_ctx = 3 * QBLK, k_ctx.shape[2]
    o = (jnp.einsum('bhgnqk,bhnkd->bhgnqd', p[..., :n_loc].astype(v.dtype), vb)
         + jnp.einsum('bhgnqk,bhkd->bhgnqd', p[..., n_loc:n_loc + n_ctx].astype(v.dtype), v_ctx))
    return o.reshape(b, hkv, g, s_len, d)


def token_mixers(h_lat, h_ctx, rope128, rope64, lambda_init, ctx_out, w_in, qn_a, kn_a, qn_b, kn_b,
                 lam_q1, lam_k1, lam_q2, lam_k2, subln_b, qn_c, kn_c, sink_c,
                 w_br_a, w_br_b, w_br_c, w_out):
    p_lat = h_lat @ w_in
    p_ctx = h_ctx @ (w_in if ctx_out else w_in[:, :KV_W])
    kA_l, vA_l, kB_l, vB_l, kC_l, vC_l = jnp.split(p_lat[..., :KV_W], KV_SPLITS, axis=-1)
    kA_c, vA_c, kB_c, vB_c, kC_c, vC_c = jnp.split(p_ctx[..., :KV_W], KV_SPLITS, axis=-1)

    kA_cx = qk_heads(kA_c, A_KV_HEADS, HEAD_DIM, kn_a, None)
    vA_cx = v_heads(vA_c, A_KV_HEADS, HEAD_DIM)
    kB_cx = qk_heads(kB_c, 2 * B_HEADS, B_QK_DIM, kn_b, None)
    vB_cx = v_heads(vB_c, B_HEADS, B_V_DIM)
    kC_cx = qk_heads(kC_c, C_KV_HEADS, HEAD_DIM, kn_c, None)
    vC_cx = v_heads(vC_c, C_KV_HEADS, HEAD_DIM)

    kA = jnp.concatenate([qk_heads(kA_l, A_KV_HEADS, HEAD_DIM, kn_a, rope128), kA_cx], axis=2)
    vA = jnp.concatenate([v_heads(vA_l, A_KV_HEADS, HEAD_DIM), vA_cx], axis=2)
    kB1, kB2 = split_pairs(jnp.concatenate([qk_heads(kB_l, 2 * B_HEADS, B_QK_DIM, kn_b, rope64), kB_cx], axis=2))
    vB = jnp.concatenate([v_heads(vB_l, B_HEADS, B_V_DIM), vB_cx], axis=2)
    kC_lat = qk_heads(kC_l, C_KV_HEADS, HEAD_DIM, kn_c, rope128)
    vC_lat = v_heads(vC_l, C_KV_HEADS, HEAD_DIM)

    lam = (jnp.exp(jnp.sum(lam_q1.astype(jnp.float32) * lam_k1.astype(jnp.float32)))
           - jnp.exp(jnp.sum(lam_q2.astype(jnp.float32) * lam_k2.astype(jnp.float32))) + lambda_init)
    sink = sink_c.reshape(C_KV_HEADS, C_Q_HEADS // C_KV_HEADS)

    def queries(p, rope_a, rope_b):
        qA, qB, qC = jnp.split(p[..., KV_W:KV_W + Q_W], Q_SPLITS, axis=-1)
        qA = group_heads(qk_heads(qA, A_Q_HEADS, HEAD_DIM, qn_a, rope_a), A_KV_HEADS)
        qB1, qB2 = split_pairs(qk_heads(qB, 2 * B_HEADS, B_QK_DIM, qn_b, rope_b))
        qC = group_heads(qk_heads(qC, C_Q_HEADS, HEAD_DIM, qn_c, rope_a), C_KV_HEADS)
        gates = jax.nn.sigmoid(p[..., KV_W + Q_W:])
        return qA, qB1, qB2, qC, gates

    def merge(oA, oB, oC, gates):
        gA, gB, gC = jnp.split(gates, 3, axis=-1)
        oB = rms_norm(oB, subln_b) * (1.0 - lambda_init)
        m = (gA * (heads_to_tokens(oA) @ w_br_a) + gB * (heads_to_tokens(oB) @ w_br_b)
             + gC * (heads_to_tokens(oC) @ w_br_c))
        return m @ w_out

    qA, qB1, qB2, qC, g = queries(p_lat, rope128, rope64)
    oA = sweep_blocks(lambda q: gqa_block(q, kA, vA), (qA,))
    oB = sweep_blocks(lambda q1, q2: diff_block(q1, q2, kB1, kB2, vB, lam), (qB1, qB2))
    oC = window_attention(qC, kC_lat, vC_lat, kC_cx, vC_cx, sink)
    out_lat = merge(oA, oB, oC, g)
    if not ctx_out:
        return out_lat, None

    qA, qB1, qB2, qC, g = queries(p_ctx, None, None)
    kB1c, kB2c = split_pairs(kB_cx)
    oA = gqa_block(qA, kA_cx, vA_cx)
    oB = diff_block(qB1, qB2, kB1c, kB2c, vB_cx, lam)
    oC = gqa_block(qC, kC_cx, vC_cx, sink)
    out_ctx = merge(oA, oB, oC, g)
    return out_lat, out_ctx


def ec_moe(h, w_router, w_gate, w_up, w_down):
    b, n, _ = h.shape
    cap = EC_CAPACITY * n // N_EXPERTS
    aff = jax.nn.softmax((h @ w_router).astype(jnp.float32), axis=-1)
    g, idx = lax.top_k(aff.transpose(0, 2, 1), cap)
    xg = h[jnp.arange(b)[:, None], idx.reshape(b, -1)].reshape(b, N_EXPERTS, cap, -1)
    u = jnp.einsum('becd,edf->becf', xg, w_gate)
    v = jnp.einsum('becd,edf->becf', xg, w_up)
    y = jnp.einsum('becf,efd->becd', jax.nn.silu(u) * v, w_down) * g[..., None].astype(h.dtype)
    return jnp.zeros_like(h).at[jnp.arange(b)[:, None, None], idx].add(y)


def setup_inputs(seed: int = 0) -> dict:
    key = jax.random.key(seed)
    ks = jax.random.split(key, 32)

    def nrm(k, shape, scale):
        return jax.random.normal(k, shape, jnp.float32) * scale

    return {
        "x": nrm(ks[0], (BATCH, SEQ, D_MODEL), 1.0),
        "c": nrm(ks[1], (BATCH, D_MODEL), 1.0),
        "ctx": nrm(ks[2], (BATCH, CTX_LEN, D_MODEL), 1.0),
        "c_ctx": nrm(ks[3], (D_MODEL,), 1.0),
        "w_mod": nrm(ks[4], (DEPTH, D_MODEL, 6 * D_MODEL), 0.5 * D_MODEL ** -0.5),
        "b_mod": nrm(ks[5], (DEPTH, 6 * D_MODEL), 0.02),
        "norm_mix": 1.0 + nrm(ks[6], (DEPTH, D_MODEL), 0.02),
        "norm_ffn": 1.0 + nrm(ks[7], (DEPTH, D_MODEL), 0.02),
        "w_in": nrm(ks[8], (DEPTH, D_MODEL, IN_W), D_MODEL ** -0.5),
        "qn_a": 1.0 + nrm(ks[9], (DEPTH, HEAD_DIM), 0.02),
        "kn_a": 1.0 + nrm(ks[10], (DEPTH, HEAD_DIM), 0.02),
        "qn_b": 1.0 + nrm(ks[11], (DEPTH, B_QK_DIM), 0.02),
        "kn_b": 1.0 + nrm(ks[12], (DEPTH, B_QK_DIM), 0.02),
        "lam_q1": nrm(ks[13], (DEPTH, B_QK_DIM), 0.1),
        "lam_k1": nrm(ks[14], (DEPTH, B_QK_DIM), 0.1),
        "lam_q2": nrm(ks[15], (DEPTH, B_QK_DIM), 0.1),
        "lam_k2": nrm(ks[16], (DEPTH, B_QK_DIM), 0.1),
        "subln_b": 1.0 + nrm(ks[17], (DEPTH, B_V_DIM), 0.02),
        "qn_c": 1.0 + nrm(ks[18], (DEPTH, HEAD_DIM), 0.02),
        "kn_c": 1.0 + nrm(ks[19], (DEPTH, HEAD_DIM), 0.02),
        "sink_c": nrm(ks[20], (DEPTH, C_Q_HEADS), 0.5),
        "w_br_a": nrm(ks[21], (DEPTH, A_Q_W, D_MODEL), A_Q_W ** -0.5),
        "w_br_b": nrm(ks[22], (DEPTH, B_V_W, D_MODEL), B_V_W ** -0.5),
        "w_br_c": nrm(ks[23], (DEPTH, C_Q_W, D_MODEL), C_Q_W ** -0.5),
        "w_out": nrm(ks[24], (DEPTH, D_MODEL, D_MODEL), D_MODEL ** -0.5),
        "w_router": nrm(ks[25], (DEPTH, D_MODEL, N_EXPERTS), D_MODEL ** -0.5),
        "w_gate": nrm(ks[26], (DEPTH, N_EXPERTS, D_MODEL, EXPERT_FF), D_MODEL ** -0.5),
        "w_up": nrm(ks[27], (DEPTH, N_EXPERTS, D_MODEL, EXPERT_FF), D_MODEL ** -0.5),
        "w_down": nrm(ks[28], (DEPTH, N_EXPERTS, EXPERT_FF, D_MODEL), EXPERT_FF ** -0.5),
    }


def reference(x, c, ctx, c_ctx, w_mod, b_mod, norm_mix, norm_ffn, w_in, qn_a, kn_a, qn_b, kn_b,
              lam_q1, lam_k1, lam_q2, lam_k2, subln_b, qn_c, kn_c, sink_c,
              w_br_a, w_br_b, w_br_c, w_out, w_router, w_gate, w_up, w_down):
    s_len = x.shape[1]
    rows = s_len // GRID_W
    row = jnp.repeat(jnp.arange(rows, dtype=jnp.int32), GRID_W)
    col = jnp.arange(s_len, dtype=jnp.int32) % GRID_W
    rope128 = axial_rope_tables(row, col, HEAD_DIM, x.dtype)
    rope64 = axial_rope_tables(row, col, B_QK_DIM, x.dtype)
    sc = jax.nn.silu(c)
    scc = jax.nn.silu(c_ctx)
    xc = ctx
    for l in range(DEPTH):
        last = l == DEPTH - 1
        lambda_init = 0.8 - 0.6 * math.exp(-0.3 * l)
        mod = (sc @ w_mod[l] + b_mod[l])[:, None, :]
        sh1, s1, g1, sh2, s2, g2 = jnp.split(mod, 6, axis=-1)
        n_mod_ctx = 2 if last else 6
        mc = jnp.split(scc @ w_mod[l][:, :n_mod_ctx * D_MODEL] + b_mod[l][:n_mod_ctx * D_MODEL], n_mod_ctx)
        h_lat = modulate(rms_norm(x, norm_mix[l]), sh1, s1)
        h_ctx = modulate(rms_norm(xc, norm_mix[l]), mc[0], mc[1])
        o_lat, o_ctx = token_mixers(h_lat, h_ctx, rope128, rope64, lambda_init, not last, w_in[l],
                                    qn_a[l], kn_a[l], qn_b[l], kn_b[l], lam_q1[l], lam_k1[l],
                                    lam_q2[l], lam_k2[l], subln_b[l], qn_c[l], kn_c[l], sink_c[l],
                                    w_br_a[l], w_br_b[l], w_br_c[l], w_out[l])
        x = x + g1 * o_lat
        x = x + g2 * ec_moe(modulate(rms_norm(x, norm_ffn[l]), sh2, s2),
                            w_router[l], w_gate[l], w_up[l], w_down[l])
        if not last:
            xc = xc + mc[2] * o_ctx
            xc = xc + mc[5] * ec_moe(modulate(rms_norm(xc, norm_ffn[l]), mc[3], mc[4]),
                                     w_router[l], w_gate[l], w_up[l], w_down[l])
    return x
```

```python
import functools
import math

import jax
import jax.numpy as jnp
from jax import lax
from jax.experimental import pallas as pl
from jax.experimental.pallas import tpu as pltpu

F32 = jnp.float32
BF16 = jnp.bfloat16

D_MODEL = 2048
GRID_W = 64
HEAD_DIM = 128
A_Q_HEADS, A_KV_HEADS = 6, 2
B_HEADS, B_QK_DIM = 4, 64
C_Q_HEADS, C_KV_HEADS = 6, 2
WINDOW = 128
N_EXPERTS = 16
EXPERT_FF = 1024
EC_CAPACITY = 2
ROPE_THETA = 10000.0
EPS = 1e-6
NEG_INF = -1e30

A_Q_W = A_Q_HEADS * HEAD_DIM
A_KV_W = A_KV_HEADS * HEAD_DIM
B_QK_W = B_HEADS * 2 * B_QK_DIM
B_V_W = B_HEADS * HEAD_DIM
C_Q_W = C_Q_HEADS * HEAD_DIM
C_KV_W = C_KV_HEADS * HEAD_DIM
KV_W = 2 * A_KV_W + B_QK_W + B_V_W + 2 * C_KV_W
Q_W = A_Q_W + B_QK_W + C_Q_W

LANES = 128
TM = 512
TQ = 256
KC = 512
FF_TILE = 256
VMEM_LIMIT = 56 * 1024 * 1024


def _params(sem, vmem=VMEM_LIMIT):
    return pltpu.CompilerParams(dimension_semantics=sem, vmem_limit_bytes=vmem)


def _split_bf16(x):
    hi = x.astype(BF16)
    lo = (x - hi.astype(F32)).astype(BF16)
    return hi, lo


def _dot3(a, b, dims):
    ah, al = _split_bf16(a)
    bh, bl = _split_bf16(b)
    dg = lambda x, y: lax.dot_general(x, y, (dims, ((), ())), preferred_element_type=F32)
    return dg(ah, bh) + (dg(ah, bl) + dg(al, bh))


def _mod_kernel(c_ref, w_ref, b_ref, o_ref):
    c = c_ref[...]
    a = c * jax.nn.sigmoid(c)
    o_ref[0] = _dot3(a, w_ref[0], ((1,), (0,))) + b_ref[0]


def _modulation(cvec, w_mod, b_mod):
    depth, d, n = w_mod.shape
    tn = 1024
    return pl.pallas_call(
        _mod_kernel,
        grid=(depth, n // tn),
        in_specs=[pl.BlockSpec((8, d), lambda l, j: (0, 0)),
                  pl.BlockSpec((1, d, tn), lambda l, j: (l, 0, j)),
                  pl.BlockSpec((1, 1, tn), lambda l, j: (l, 0, j))],
        out_specs=pl.BlockSpec((1, 8, tn), lambda l, j: (l, 0, j)),
        out_shape=jax.ShapeDtypeStruct((depth, 8, n), F32),
        compiler_params=_params(("arbitrary", "arbitrary")),
        name="modulation",
    )(cvec, w_mod, b_mod.reshape(depth, 1, n))


def _modulated_norm(x, gain, mod, shift_row, scale_row):
    ms = jnp.mean(x * x, axis=-1, keepdims=True)
    y = x * lax.rsqrt(ms + EPS) * gain
    return y * (1.0 + mod[scale_row:scale_row + 1, :]) + mod[shift_row:shift_row + 1, :]


def _norm_kernel(x_ref, g_ref, mod_ref, h_ref):
    h_ref[...] = _modulated_norm(x_ref[...], g_ref[...], mod_ref[0], 0, 1).astype(BF16)


def _group_of_tile(i, tiles_per_batch, n_batch):
    return jnp.minimum(i // tiles_per_batch, n_batch)


def _norm_mix(x_all, gain, mod3, tiles_per_batch, n_batch):
    r, d = x_all.shape
    return pl.pallas_call(
        _norm_kernel,
        grid=(r // TM,),
        in_specs=[pl.BlockSpec((TM, d), lambda i: (i, 0)),
                  pl.BlockSpec((1, d), lambda i: (0, 0)),
                  pl.BlockSpec((1, 8, d), lambda i: (_group_of_tile(i, tiles_per_batch, n_batch), 0, 0))],
        out_specs=pl.BlockSpec((TM, d), lambda i: (i, 0)),
        out_shape=jax.ShapeDtypeStruct((r, d), BF16),
        compiler_params=_params(("arbitrary",)),
        name="norm_mix",
    )(x_all, gain.reshape(1, d), mod3)


def _rope_tables(s_len, dh, pad_rows):
    d_ax = dh // 2
    inv = ROPE_THETA ** (-jnp.arange(0, d_ax, 2, dtype=F32) / d_ax)
    t = jnp.arange(s_len, dtype=jnp.int32)
    fr = (t // GRID_W).astype(F32)[:, None] * inv
    fc = (t % GRID_W).astype(F32)[:, None] * inv
    ang = jnp.concatenate([fr, fr, fc, fc], axis=-1)
    cos, sin = jnp.cos(ang), jnp.sin(ang)
    quarter = jnp.arange(dh) // (dh // 4)
    sin_a = jnp.where(quarter % 2 == 0, -sin, 0.0)
    sin_b = jnp.where(quarter % 2 == 1, sin, 0.0)
    reps = LANES // dh

    def finish(tab, ident):
        tab = jnp.tile(tab, (1, reps))
        return jnp.concatenate([tab, jnp.full((pad_rows, LANES), ident, F32)], axis=0)

    return finish(cos, 1.0), finish(sin_a, 0.0), finish(sin_b, 0.0)


def _norm_rope(t, gain, cos, sin_a, sin_b, group, scale):
    sq = t * t
    if group == LANES:
        ms = jnp.mean(sq, axis=-1, keepdims=True)
    else:
        lane = lax.broadcasted_iota(jnp.int32, t.shape, 1)
        low = lane < group
        s_lo = jnp.sum(jnp.where(low, sq, 0.0), axis=-1, keepdims=True)
        s_hi = jnp.sum(jnp.where(low, 0.0, sq), axis=-1, keepdims=True)
        ms = jnp.where(low, s_lo, s_hi) * (1.0 / group)
    y = t * lax.rsqrt(ms + EPS) * gain
    q = group // 4
    r = y * cos + pltpu.roll(y, LANES - q, 1) * sin_a + pltpu.roll(y, q, 1) * sin_b
    if scale != 1.0:
        r = r * scale
    return r


def _kv_kernel(h_ref, w_ref, c128, a128, b128, c64, a64, b64, gka, gkb, gkc,
               ka_ref, kb_ref, kc_ref, vta_ref, vtb_ref, vtc_ref):
    h = h_ref[...]

    def mm(c0):
        return jnp.dot(h, w_ref[:, c0:c0 + 2 * LANES], preferred_element_type=F32)

    def keys(c0, out_ref, o0, gain, cos, sa, sb, group):
        acc = mm(c0)
        for j in range(2):
            blk = acc[:, j * LANES:(j + 1) * LANES]
            out_ref[:, o0 + j * LANES:o0 + (j + 1) * LANES] = _norm_rope(
                blk, gain[...], cos[...], sa[...], sb[...], group, 1.0).astype(BF16)

    def values(c0, out_ref, h0):
        acc = mm(c0)
        for j in range(2):
            out_ref[h0 + j] = acc[:, j * LANES:(j + 1) * LANES].T.astype(BF16)

    c = 0
    keys(c, ka_ref, 0, gka, c128, a128, b128, HEAD_DIM)
    c += A_KV_W
    values(c, vta_ref, 0)
    c += A_KV_W
    for j in range(B_QK_W // (2 * LANES)):
        keys(c, kb_ref, j * 2 * LANES, gkb, c64, a64, b64, B_QK_DIM)
        c += 2 * LANES
    for j in range(B_V_W // (2 * LANES)):
        values(c, vtb_ref, 2 * j)
        c += 2 * LANES
    keys(c, kc_ref, 0, gkc, c128, a128, b128, HEAD_DIM)
    c += C_KV_W
    values(c, vtc_ref, 0)


def _table_specs(pos_tiles, n_lat_tiles):
    spec = pl.BlockSpec((TM, LANES), lambda i: (jnp.where(i < n_lat_tiles, i % pos_tiles, pos_tiles), 0))
    return [spec] * 6


def _kv_proj(h, w_kv, tabs, gka, gkb, gkc, pos_tiles, n_lat_tiles):
    r, d = h.shape
    nt = r // TM
    vec = pl.BlockSpec((1, LANES), lambda i: (0, 0))
    return pl.pallas_call(
        _kv_kernel,
        grid=(nt,),
        in_specs=[pl.BlockSpec((TM, d), lambda i: (i, 0)),
                  pl.BlockSpec((d, KV_W), lambda i: (0, 0))] + _table_specs(pos_tiles, n_lat_tiles) + [vec] * 3,
        out_specs=[pl.BlockSpec((TM, A_KV_W), lambda i: (i, 0)),
                   pl.BlockSpec((TM, B_QK_W), lambda i: (i, 0)),
                   pl.BlockSpec((TM, C_KV_W), lambda i: (i, 0)),
                   pl.BlockSpec((A_KV_HEADS, LANES, TM), lambda i: (0, 0, i)),
                   pl.BlockSpec((B_HEADS, LANES, TM), lambda i: (0, 0, i)),
                   pl.BlockSpec((C_KV_HEADS, LANES, TM), lambda i: (0, 0, i))],
        out_shape=[jax.ShapeDtypeStruct((r, A_KV_W), BF16),
                   jax.ShapeDtypeStruct((r, B_QK_W), BF16),
                   jax.ShapeDtypeStruct((r, C_KV_W), BF16),
                   jax.ShapeDtypeStruct((A_KV_HEADS, LANES, r), BF16),
                   jax.ShapeDtypeStruct((B_HEADS, LANES, r), BF16),
                   jax.ShapeDtypeStruct((C_KV_HEADS, LANES, r), BF16)],
        compiler_params=_params(("arbitrary",)),
        name="kv_proj",
    )(h, w_kv, *tabs, gka, gkb, gkc)


def _q_kernel(h_ref, w_ref, c128, a128, b128, c64, a64, b64, gqa, gqb, gqc, qa_ref, qb_ref, qc_ref):
    h = h_ref[...]

    def queries(c0, out_ref, o0, gain, cos, sa, sb, group):
        acc = jnp.dot(h, w_ref[:, c0:c0 + 2 * LANES], preferred_element_type=F32)
        for j in range(2):
            blk = acc[:, j * LANES:(j + 1) * LANES]
            out_ref[:, o0 + j * LANES:o0 + (j + 1) * LANES] = _norm_rope(
                blk, gain[...], cos[...], sa[...], sb[...], group, group ** -0.5).astype(BF16)

    c = 0
    for j in range(A_Q_W // (2 * LANES)):
        queries(c, qa_ref, j * 2 * LANES, gqa, c128, a128, b128, HEAD_DIM)
        c += 2 * LANES
    for j in range(B_QK_W // (2 * LANES)):
        queries(c, qb_ref, j * 2 * LANES, gqb, c64, a64, b64, B_QK_DIM)
        c += 2 * LANES
    for j in range(C_Q_W // (2 * LANES)):
        queries(c, qc_ref, j * 2 * LANES, gqc, c128, a128, b128, HEAD_DIM)
        c += 2 * LANES


def _q_proj(h, w_q, tabs, gqa, gqb, gqc, pos_tiles, n_lat_tiles, nt):
    d = h.shape[1]
    rows = nt * TM
    vec = pl.BlockSpec((1, LANES), lambda i: (0, 0))
    return pl.pallas_call(
        _q_kernel,
        grid=(nt,),
        in_specs=[pl.BlockSpec((TM, d), lambda i: (i, 0)),
                  pl.BlockSpec((d, Q_W), lambda i: (0, 0))] + _table_specs(pos_tiles, n_lat_tiles) + [vec] * 3,
        out_specs=[pl.BlockSpec((TM, A_Q_W), lambda i: (i, 0)),
                   pl.BlockSpec((TM, B_QK_W), lambda i: (i, 0)),
                   pl.BlockSpec((TM, C_Q_W), lambda i: (i, 0))],
        out_shape=[jax.ShapeDtypeStruct((rows, A_Q_W), BF16),
                   jax.ShapeDtypeStruct((rows, B_QK_W), BF16),
                   jax.ShapeDtypeStruct((rows, C_Q_W), BF16)],
        compiler_params=_params(("arbitrary",)),
        name="q_proj",
    )(h, w_q, *tabs, gqa, gqb, gqc)


def _gate_kernel(h_ref, w_ref, g_ref):
    h = h_ref[...]
    for c in range(0, D_MODEL, 2 * LANES):
        z = jnp.dot(h, w_ref[:, c:c + 2 * LANES], preferred_element_type=F32)
        g_ref[:, c:c + 2 * LANES] = jax.nn.sigmoid(z).astype(BF16)


def _gate_proj(h, w_g, nt):
    d = h.shape[1]
    return pl.pallas_call(
        _gate_kernel,
        grid=(3, nt),
        in_specs=[pl.BlockSpec((TM, d), lambda j, i: (i, 0)),
                  pl.BlockSpec((d, D_MODEL), lambda j, i: (0, j))],
        out_specs=pl.BlockSpec((TM, D_MODEL), lambda j, i: (i, j)),
        out_shape=jax.ShapeDtypeStruct((nt * TM, 3 * D_MODEL), BF16),
        compiler_params=_params(("arbitrary", "arbitrary")),
        name="gate_proj",
    )(h, w_g)


def _attn_kernel(*refs, heads, dual, tq, lat_len, n_lat_steps, window, has_sink, lambda_init):
    it = iter(refs)
    q_ref, klat_ref, vlat_ref, kctx_ref, vctx_ref = next(it), next(it), next(it), next(it), next(it)
    if dual:
        lq1, lk1, lq2, lk2, sub_ref = next(it), next(it), next(it), next(it), next(it)
    if has_sink:
        sink_ref = next(it)
    o_ref = next(it)
    qe_ref, m_ref, l_ref, acc_ref = next(it), next(it), next(it), next(it)

    step = pl.program_id(2)
    is_latent = step < n_lat_steps
    ncol = (2 if dual else heads) * tq
    if dual:
        q = q_ref[...]
        lane = lax.broadcasted_iota(jnp.int32, q.shape, 1)
        zero = jnp.zeros_like(q)
        qe_ref[0:tq, :] = jnp.where(lane < B_QK_DIM, q, zero)
        qe_ref[tq:2 * tq, :] = jnp.where(lane < B_QK_DIM, zero, q)
    else:
        for hh in range(heads):
            qe_ref[hh * tq:(hh + 1) * tq, :] = q_ref[:, hh * LANES:(hh + 1) * LANES]

    if has_sink:
        g = pl.program_id(1)
        for hh in range(heads):
            m_ref[:, hh * tq:(hh + 1) * tq] = jnp.full((1, tq), sink_ref[g * heads + hh], F32)
        l_ref[...] = jnp.ones((1, ncol), F32)
    else:
        m_ref[...] = jnp.full((1, ncol), NEG_INF, F32)
        l_ref[...] = jnp.zeros((1, ncol), F32)
    acc_ref[...] = jnp.zeros((LANES, ncol), F32)

    def chunk(k, vt, valid=None):
        s = lax.dot_general(k, qe_ref[...], (((1,), (1,)), ((), ())), preferred_element_type=F32)
        if valid is not None:
            s = jnp.where(valid, s, NEG_INF)
        m_old = m_ref[...]
        m_new = jnp.maximum(m_old, jnp.max(s, axis=0, keepdims=True))
        alpha = jnp.exp(m_old - m_new)
        p = jnp.exp(s - m_new)
        l_ref[...] = alpha * l_ref[...] + jnp.sum(p, axis=0, keepdims=True)
        acc_ref[...] = alpha * acc_ref[...] + jnp.dot(vt, p.astype(BF16), preferred_element_type=F32)
        m_ref[...] = m_new

    chunk(kctx_ref[...], vctx_ref[0])

    if not window:
        def body(c, carry):
            start = pl.multiple_of(c * KC, KC)
            chunk(klat_ref[pl.ds(start, KC), :], vlat_ref[0, :, pl.ds(start, KC)])
            return carry
        lax.fori_loop(0, jnp.where(is_latent, lat_len // KC, 0), body, 0)
    else:
        @pl.when(is_latent)
        def _():
            q0 = step * tq
            kk = lax.broadcasted_iota(jnp.int32, (WINDOW, ncol), 0)
            qq = lax.broadcasted_iota(jnp.int32, (WINDOW, ncol), 1) % tq
            rel = kk - qq
            nblk = lat_len // WINDOW
            for j in range(-1, tq // WINDOW + 1):
                blk = q0 // WINDOW + j
                blk_c = jnp.clip(blk, 0, nblk - 1)
                start = pl.multiple_of(blk_c * WINDOW, WINDOW)
                dist = rel + j * WINDOW
                in_range = jnp.logical_and(blk >= 0, blk < nblk)
                valid = jnp.logical_and(jnp.abs(dist) <= WINDOW, in_range)
                chunk(klat_ref[pl.ds(start, WINDOW), :], vlat_ref[0, :, pl.ds(start, WINDOW)], valid)

    o = acc_ref[...] * (1.0 / l_ref[...])
    if dual:
        s1 = jnp.sum(lq1[...] * lk1[...], axis=-1, keepdims=True)
        s2 = jnp.sum(lq2[...] * lk2[...], axis=-1, keepdims=True)
        lam = jnp.exp(s1) - jnp.exp(s2) + lambda_init
        od = (o[:, 0:tq] - lam * o[:, tq:2 * tq]).T
        ms = jnp.mean(od * od, axis=-1, keepdims=True)
        od = od * lax.rsqrt(ms + EPS) * sub_ref[...] * (1.0 - lambda_init)
        o_ref[...] = od.astype(BF16)
    else:
        for hh in range(heads):
            o_ref[:, hh * LANES:(hh + 1) * LANES] = o[:, hh * tq:(hh + 1) * tq].T.astype(BF16)


def _attention(q, k_all, vt_all, *, n_batch, s_len, ctx_len, ctx_queries, heads, n_groups, dual, window,
               lambda_init=0.0, lam_params=None, subln=None, sink=None):
    assert ctx_len == TQ
    width = heads * LANES
    nq = s_len // TQ
    ctx_blk0 = n_batch * s_len // ctx_len
    ncol = (2 if dual else heads) * TQ
    qmap = lambda b, g, i: (jnp.where(i < nq, b * nq + i, ctx_blk0 + b), g)

    in_specs = [pl.BlockSpec((TQ, width), qmap),
                pl.BlockSpec((s_len, LANES), lambda b, g, i: (b, g)),
                pl.BlockSpec((1, LANES, s_len), lambda b, g, i: (g, 0, b)),
                pl.BlockSpec((ctx_len, LANES), lambda b, g, i: (ctx_blk0 + b, g)),
                pl.BlockSpec((1, LANES, ctx_len), lambda b, g, i: (g, 0, ctx_blk0 + b))]
    args = [q, k_all, vt_all, k_all, vt_all]
    if dual:
        in_specs += [pl.BlockSpec((1, B_QK_DIM), lambda b, g, i: (0, 0))] * 4
        in_specs += [pl.BlockSpec((1, LANES), lambda b, g, i: (0, 0))]
        args += [p.reshape(1, B_QK_DIM) for p in lam_params] + [subln.reshape(1, LANES)]
    if sink is not None:
        in_specs += [pl.BlockSpec(memory_space=pltpu.SMEM)]
        args += [sink]

    kern = functools.partial(_attn_kernel, heads=heads, dual=dual, tq=TQ, lat_len=s_len, n_lat_steps=nq,
                             window=window, has_sink=sink is not None, lambda_init=lambda_init)
    return pl.pallas_call(
        kern,
        grid=(n_batch, n_groups, nq + (1 if ctx_queries else 0)),
        in_specs=in_specs,
        out_specs=pl.BlockSpec((TQ, width), qmap),
        out_shape=jax.ShapeDtypeStruct((q.shape[0], n_groups * width), BF16),
        scratch_shapes=[pltpu.VMEM((ncol, LANES), BF16), pltpu.VMEM((1, ncol), F32),
                        pltpu.VMEM((1, ncol), F32), pltpu.VMEM((LANES, ncol), F32)],
        compiler_params=_params(("arbitrary", "arbitrary", "arbitrary")),
        name="attention",
    )(*args)


def _merge_kernel(oa_ref, ob_ref, oc_ref, g_ref, wa_ref, wb_ref, wc_ref, m_ref):
    oa, ob, oc = oa_ref[...], ob_ref[...], oc_ref[...]
    step = 4 * LANES
    for c in range(0, D_MODEL, step):
        sl = slice(c, c + step)
        m = g_ref[:, c:c + step].astype(F32) * jnp.dot(oa, wa_ref[:, sl], preferred_element_type=F32)
        m = m + g_ref[:, D_MODEL + c:D_MODEL + c + step].astype(F32) * jnp.dot(
            ob, wb_ref[:, sl], preferred_element_type=F32)
        m = m + g_ref[:, 2 * D_MODEL + c:2 * D_MODEL + c + step].astype(F32) * jnp.dot(
            oc, wc_ref[:, sl], preferred_element_type=F32)
        m_ref[:, sl] = m.astype(BF16)


def _merge(oa, ob, oc, gates, wa, wb, wc, nt):
    const = lambda i: (0, 0)
    return pl.pallas_call(
        _merge_kernel,
        grid=(nt,),
        in_specs=[pl.BlockSpec((TM, A_Q_W), lambda i: (i, 0)),
                  pl.BlockSpec((TM, B_V_W), lambda i: (i, 0)),
                  pl.BlockSpec((TM, C_Q_W), lambda i: (i, 0)),
                  pl.BlockSpec((TM, 3 * D_MODEL), lambda i: (i, 0)),
                  pl.BlockSpec((A_Q_W, D_MODEL), const),
                  pl.BlockSpec((B_V_W, D_MODEL), const),
                  pl.BlockSpec((C_Q_W, D_MODEL), const)],
        out_specs=pl.BlockSpec((TM, D_MODEL), lambda i: (i, 0)),
        out_shape=jax.ShapeDtypeStruct((nt * TM, D_MODEL), BF16),
        compiler_params=_params(("arbitrary",)),
        name="merge",
    )(oa, ob, oc, gates, wa, wb, wc)


def _out_kernel(m_ref, w_ref, x_ref, mod_ref, o_ref):
    m = m_ref[...]
    step = 4 * LANES
    for c in range(0, D_MODEL, step):
        o = jnp.dot(m, w_ref[:, c:c + step], preferred_element_type=F32)
        o_ref[:, c:c + step] = x_ref[:, c:c + step] + mod_ref[0, 2:3, c:c + step] * o


def _out_proj(m, w_out, x_all, mod3, nt, tiles_per_batch, n_batch):
    return pl.pallas_call(
        _out_kernel,
        grid=(nt,),
        in_specs=[pl.BlockSpec((TM, D_MODEL), lambda i: (i, 0)),
                  pl.BlockSpec((D_MODEL, D_MODEL), lambda i: (0, 0)),
                  pl.BlockSpec((TM, D_MODEL), lambda i: (i, 0)),
                  pl.BlockSpec((1, 8, D_MODEL), lambda i: (_group_of_tile(i, tiles_per_batch, n_batch), 0, 0))],
        out_specs=pl.BlockSpec((TM, D_MODEL), lambda i: (i, 0)),
        out_shape=jax.ShapeDtypeStruct((nt * TM, D_MODEL), F32),
        compiler_params=_params(("arbitrary",)),
        name="out_proj",
    )(m, w_out, x_all, mod3)


def _norm_router_kernel(x_ref, g_ref, mod_ref, wr_ref, h_ref, aff_ref):
    h = _modulated_norm(x_ref[...], g_ref[...], mod_ref[0], 3, 4)
    h_ref[...] = h
    logits = _dot3(wr_ref[...], h, ((1,), (1,)))
    e = jnp.exp(logits - jnp.max(logits, axis=0, keepdims=True))
    aff_ref[...] = e / jnp.sum(e, axis=0, keepdims=True)


def _norm_router(x_all, gain, mod3, w_router_t, nt, tiles_per_batch, n_batch):
    d = x_all.shape[1]
    return pl.pallas_call(
        _norm_router_kernel,
        grid=(nt,),
        in_specs=[pl.BlockSpec((TM, d), lambda i: (i, 0)),
                  pl.BlockSpec((1, d), lambda i: (0, 0)),
                  pl.BlockSpec((1, 8, d), lambda i: (_group_of_tile(i, tiles_per_batch, n_batch), 0, 0)),
                  pl.BlockSpec((N_EXPERTS, d), lambda i: (0, 0))],
        out_specs=[pl.BlockSpec((TM, d), lambda i: (i, 0)),
                   pl.BlockSpec((N_EXPERTS, TM), lambda i: (0, i))],
        out_shape=[jax.ShapeDtypeStruct((nt * TM, d), F32),
                   jax.ShapeDtypeStruct((N_EXPERTS, nt * TM), F32)],
        compiler_params=_params(("arbitrary",)),
        name="norm_router",
    )(x_all, gain.reshape(1, d), mod3, w_router_t)


def _prefix_count(mask, n):
    ii = lax.broadcasted_iota(jnp.int32, (LANES, LANES), 0)
    jj = lax.broadcasted_iota(jnp.int32, (LANES, LANES), 1)
    tri = jnp.where(ii <= jj, 1.0, 0.0).astype(BF16)
    m = jnp.where(mask, 1.0, 0.0).astype(BF16)
    outs = []
    off = jnp.zeros((mask.shape[0], 1), F32)
    for b in range(n // LANES):
        c = jnp.dot(m[:, b * LANES:(b + 1) * LANES], tri, preferred_element_type=F32) + off
        outs.append(c)
        off = c[:, LANES - 1:LANES]
    return jnp.concatenate(outs, axis=1)


def _route_kernel(aff_ref, idx_ref, gate_ref, csel_ref, *, n, cap, row_base):
    a = aff_ref[...]
    bits = pltpu.bitcast(a, jnp.int32)
    kf = float(cap)

    def search(i, t):
        cand = t | jnp.left_shift(jnp.int32(1), 30 - i)
        cnt = jnp.sum(jnp.where(bits >= cand, 1.0, 0.0), axis=1, keepdims=True)
        return jnp.where(cnt >= kf, cand, t)

    thr = lax.fori_loop(0, 31, search, jnp.zeros((N_EXPERTS, 1), jnp.int32))
    above = bits > thr
    equal = bits == thr
    need = kf - jnp.sum(jnp.where(above, 1.0, 0.0), axis=1, keepdims=True)
    chosen = jnp.logical_or(above, jnp.logical_and(equal, _prefix_count(equal, n) <= need))
    csel_ref[...] = jnp.where(chosen, _prefix_count(chosen, n), 0.0)

    tok = lax.broadcasted_iota(jnp.int32, (1, n), 1).astype(F32)
    sub = min(cap, 64)
    base = row_base + pl.program_id(0) * n

    def per_expert(e, carry):
        ce = csel_ref[pl.ds(e, 1), :]
        ae = aff_ref[pl.ds(e, 1), :]
        for s0 in range(0, cap, sub):
            slot = (lax.broadcasted_iota(jnp.int32, (sub, n), 0) + (s0 + 1)).astype(F32)
            hit = ce == slot
            tsel = jnp.sum(jnp.where(hit, tok, 0.0), axis=1, keepdims=True)
            gsel = jnp.sum(jnp.where(hit, ae, 0.0), axis=1, keepdims=True)
            idx_ref[0, e, s0:s0 + sub, :] = tsel.astype(jnp.int32) + base
            gate_ref[0, e, s0:s0 + sub, :] = gsel
        return carry

    lax.fori_loop(0, N_EXPERTS, per_expert, 0)


def _route(aff_t, n_sets, n, cap, first_block, row_base):
    return pl.pallas_call(
        functools.partial(_route_kernel, n=n, cap=cap, row_base=row_base),
        grid=(n_sets,),
        in_specs=[pl.BlockSpec((N_EXPERTS, n), lambda s: (0, first_block + s))],
        out_specs=[pl.BlockSpec((1, N_EXPERTS, cap, 1), lambda s: (s, 0, 0, 0)),
                   pl.BlockSpec((1, N_EXPERTS, cap, 1), lambda s: (s, 0, 0, 0))],
        out_shape=[jax.ShapeDtypeStruct((n_sets, N_EXPERTS, cap, 1), jnp.int32),
                   jax.ShapeDtypeStruct((n_sets, N_EXPERTS, cap, 1), F32)],
        scratch_shapes=[pltpu.VMEM((N_EXPERTS, n), F32)],
        compiler_params=_params(("arbitrary",)),
        name="route",
    )(aff_t)


def _ffn_kernel(rows_ref, h_hbm, wg_ref, wu_ref, wd_ref, gate_ref, mod_ref, x_in_hbm, x_hbm,
                buf, xg, yacc, sem, *, nr, n_ff, groups):
    del x_in_hbm
    e = pl.program_id(0)
    f = pl.program_id(1)
    base = e * nr

    def row_copies(hbm, to_hbm):
        def copy(hbm_rows, buf_rows):
            if to_hbm:
                return pltpu.make_async_copy(buf_rows, hbm_rows, sem)
            return pltpu.make_async_copy(hbm_rows, buf_rows, sem)

        def issue(i, carry):
            r = rows_ref[base + i]
            copy(hbm.at[pl.ds(r, 1), :], buf.at[pl.ds(i, 1), :]).start()
            return carry

        lax.fori_loop(0, nr, issue, 0)
        copy(hbm.at[pl.ds(0, nr), :], buf).wait()

    @pl.when(f == 0)
    def _():
        row_copies(h_hbm, False)
        xg[...] = buf[...].astype(BF16)

    x = xg[...]
    u = jnp.dot(x, wg_ref[0].astype(BF16), preferred_element_type=F32)
    v = jnp.dot(x, wu_ref[0].astype(BF16), preferred_element_type=F32)
    act = (u * jax.nn.sigmoid(u) * v).astype(BF16)
    contrib = jnp.dot(act, wd_ref[0].astype(BF16), preferred_element_type=F32)

    @pl.when(f == 0)
    def _():
        yacc[...] = contrib

    @pl.when(f > 0)
    def _():
        yacc[...] += contrib

    @pl.when(f == n_ff - 1)
    def _():
        row_copies(x_hbm, False)
        for (r0, r1, grp) in groups:
            buf[r0:r1, :] += yacc[r0:r1, :] * gate_ref[0, r0:r1, :] * mod_ref[grp, 5:6, :]
        row_copies(x_hbm, True)


def _expert_ffn(rows, h2, w_gate, w_up, w_down, gates, mod3, x_all, nr, groups):
    d = h2.shape[1]
    n_ff = EXPERT_FF // FF_TILE
    kern = functools.partial(_ffn_kernel, nr=nr, n_ff=n_ff, groups=groups)
    return pl.pallas_call(
        kern,
        grid_spec=pltpu.PrefetchScalarGridSpec(
            num_scalar_prefetch=1,
            grid=(N_EXPERTS, n_ff),
            in_specs=[pl.BlockSpec(memory_space=pl.ANY),
                      pl.BlockSpec((1, d, FF_TILE), lambda e, f, rows: (e, 0, f)),
                      pl.BlockSpec((1, d, FF_TILE), lambda e, f, rows: (e, 0, f)),
                      pl.BlockSpec((1, FF_TILE, d), lambda e, f, rows: (e, f, 0)),
                      pl.BlockSpec((1, nr, 1), lambda e, f, rows: (e, 0, 0)),
                      pl.BlockSpec((3, 8, d), lambda e, f, rows: (0, 0, 0)),
                      pl.BlockSpec(memory_space=pl.ANY)],
            out_specs=pl.BlockSpec(memory_space=pl.ANY),
            scratch_shapes=[pltpu.VMEM((nr, d), F32), pltpu.VMEM((nr, d), BF16), pltpu.VMEM((nr, d), F32),
                            pltpu.SemaphoreType.DMA(())]),
        out_shape=jax.ShapeDtypeStruct(x_all.shape, F32),
        input_output_aliases={7: 0},
        compiler_params=_params(("arbitrary", "arbitrary")),
        name="expert_ffn",
    )(rows, h2, w_gate, w_up, w_down, gates, mod3, x_all)


def kernel(x, c, ctx, c_ctx, w_mod, b_mod, norm_mix, norm_ffn, w_in, qn_a, kn_a, qn_b, kn_b,
           lam_q1, lam_k1, lam_q2, lam_k2, subln_b, qn_c, kn_c, sink_c,
           w_br_a, w_br_b, w_br_c, w_out, w_router, w_gate, w_up, w_down):
    n_batch, s_len, d = x.shape
    ctx_len = ctx.shape[1]
    depth = w_mod.shape[0]
    lat_rows = n_batch * s_len
    tiles_per_batch = s_len // TM
    n_lat_tiles = lat_rows // TM
    assert s_len % TM == 0 and (n_batch * ctx_len) == TM and s_len % KC == 0 and d == D_MODEL

    cvec = jnp.zeros((8, d), F32).at[:n_batch].set(c).at[n_batch].set(c_ctx)
    mod_all = _modulation(cvec, w_mod, b_mod)

    tabs = _rope_tables(s_len, HEAD_DIM, TM) + _rope_tables(s_len, B_QK_DIM, TM)
    x_all = jnp.concatenate([x.reshape(lat_rows, d), ctx.reshape(n_batch * ctx_len, d)], axis=0)

    cap_lat = EC_CAPACITY * s_len // N_EXPERTS
    cap_ctx = EC_CAPACITY * ctx_len // N_EXPERTS
    lane_vec = lambda v: v.reshape(1, -1) if v.shape[-1] == LANES else jnp.tile(v, LANES // v.shape[-1]).reshape(1, LANES)

    for l in range(depth):
        last = l == depth - 1
        lambda_init = 0.8 - 0.6 * math.exp(-0.3 * l)
        nt = n_lat_tiles if last else n_lat_tiles + 1
        mod3 = jnp.pad(mod_all[l, :n_batch + 1].reshape(n_batch + 1, 6, d), ((0, 0), (0, 2), (0, 0)))

        h = _norm_mix(x_all, norm_mix[l], mod3, tiles_per_batch, n_batch)
        w_in_l = w_in[l].astype(BF16)
        ka, kb, kc, vta, vtb, vtc = _kv_proj(h, w_in_l[:, :KV_W], tabs, lane_vec(kn_a[l]), lane_vec(kn_b[l]),
                                             lane_vec(kn_c[l]), tiles_per_batch, n_lat_tiles)
        qa, qb, qc = _q_proj(h, w_in_l[:, KV_W:KV_W + Q_W], tabs, lane_vec(qn_a[l]), lane_vec(qn_b[l]),
                             lane_vec(qn_c[l]), tiles_per_batch, n_lat_tiles, nt)
        gates = _gate_proj(h, w_in_l[:, KV_W + Q_W:], nt)

        common = dict(n_batch=n_batch, s_len=s_len, ctx_len=ctx_len, ctx_queries=not last)
        lam_params = (lam_q1[l], lam_k1[l], lam_q2[l], lam_k2[l])
        oa = _attention(qa, ka, vta, heads=A_Q_HEADS // A_KV_HEADS, n_groups=A_KV_HEADS, dual=False, window=False,
                        **common)
        ob = _attention(qb, kb, vtb, heads=1, n_groups=B_HEADS, dual=True, window=False, lambda_init=lambda_init,
                        lam_params=lam_params, subln=subln_b[l], **common)
        oc = _attention(qc, kc, vtc, heads=C_Q_HEADS // C_KV_HEADS, n_groups=C_KV_HEADS, dual=False, window=True,
                        sink=sink_c[l], **common)

        m = _merge(oa, ob, oc, gates, w_br_a[l].astype(BF16), w_br_b[l].astype(BF16), w_br_c[l].astype(BF16), nt)
        x_mid = _out_proj(m, w_out[l].astype(BF16), x_all, mod3, nt, tiles_per_batch, n_batch)

        h2, aff_t = _norm_router(x_mid, norm_ffn[l], mod3, w_router[l].T, nt, tiles_per_batch, n_batch)
        idx_lat, gate_lat = _route(aff_t, n_batch, s_len, cap_lat, 0, 0)
        idx_parts = [idx_lat[b, :, :, 0] for b in range(n_batch)]
        gate_parts = [gate_lat[b] for b in range(n_batch)]
        groups = [(b * cap_lat, (b + 1) * cap_lat, b) for b in range(n_batch)]
        if not last:
            idx_ctx, gate_ctx = _route(aff_t, n_batch, ctx_len, cap_ctx, lat_rows // ctx_len, lat_rows)
            idx_parts += [idx_ctx[b, :, :, 0] for b in range(n_batch)]
            gate_parts += [gate_ctx[b] for b in range(n_batch)]
            groups.append((n_batch * cap_lat, n_batch * (cap_lat + cap_ctx), n_batch))
        rows = jnp.concatenate(idx_parts, axis=1)
        gate_rows = jnp.concatenate(gate_parts, axis=1)
        nr = rows.shape[1]
        x_all = _expert_ffn(rows.reshape(-1), h2, w_gate[l], w_up[l], w_down[l], gate_rows, mod3, x_mid,
                            nr, tuple(groups))

    return x_all[:lat_rows].reshape(n_batch, s_len, d)
```

```python
import functools
import math

import jax
import jax.numpy as jnp
from jax import lax
from jax.experimental import pallas as pl
from jax.experimental.pallas import tpu as pltpu

F32 = jnp.float32
BF16 = jnp.bfloat16

D_MODEL = 2048
GRID_W = 64
HEAD_DIM = 128
A_Q_HEADS, A_KV_HEADS = 6, 2
B_HEADS, B_QK_DIM = 4, 64
C_Q_HEADS, C_KV_HEADS = 6, 2
WINDOW = 128
N_EXPERTS = 16
EXPERT_FF = 1024
EC_CAPACITY = 2
ROPE_THETA = 10000.0
EPS = 1e-6
NEG_INF = -1e30

A_Q_W = A_Q_HEADS * HEAD_DIM
A_KV_W = A_KV_HEADS * HEAD_DIM
B_QK_W = B_HEADS * 2 * B_QK_DIM
B_V_W = B_HEADS * HEAD_DIM
C_Q_W = C_Q_HEADS * HEAD_DIM
C_KV_W = C_KV_HEADS * HEAD_DIM
KV_W = 2 * A_KV_W + B_QK_W + B_V_W + 2 * C_KV_W
Q_W = A_Q_W + B_QK_W + C_Q_W

LANES = 128
TM = 512
TQ = 256
KC = 512
FF_TILE = 256
VMEM_LIMIT = 56 * 1024 * 1024
LOG2E = 1.4426950408889634
SAFE_LOG2_BOUND = 60.0
BOUND_MARGIN = 1.03


def _params(sem, vmem=VMEM_LIMIT):
    return pltpu.CompilerParams(dimension_semantics=sem, vmem_limit_bytes=vmem)


def _split_bf16(x):
    hi = x.astype(BF16)
    lo = (x - hi.astype(F32)).astype(BF16)
    return hi, lo


def _dot3(a, b, dims):
    ah, al = _split_bf16(a)
    bh, bl = _split_bf16(b)
    dg = lambda x, y: lax.dot_general(x, y, (dims, ((), ())), preferred_element_type=F32)
    return dg(ah, bh) + (dg(ah, bl) + dg(al, bh))


def _mod_kernel(c_ref, w_ref, b_ref, o_ref):
    c = c_ref[...]
    a = c * jax.nn.sigmoid(c)
    o_ref[0] = _dot3(a, w_ref[0], ((1,), (0,))) + b_ref[0]


def _modulation(cvec, w_mod, b_mod):
    depth, d, n = w_mod.shape
    tn = 1024
    return pl.pallas_call(
        _mod_kernel,
        grid=(depth, n // tn),
        in_specs=[pl.BlockSpec((8, d), lambda l, j: (0, 0)),
                  pl.BlockSpec((1, d, tn), lambda l, j: (l, 0, j)),
                  pl.BlockSpec((1, 1, tn), lambda l, j: (l, 0, j))],
        out_specs=pl.BlockSpec((1, 8, tn), lambda l, j: (l, 0, j)),
        out_shape=jax.ShapeDtypeStruct((depth, 8, n), F32),
        compiler_params=_params(("arbitrary", "arbitrary")),
        name="modulation",
    )(cvec, w_mod, b_mod.reshape(depth, 1, n))


def _modulated_norm(x, gain, mod, shift_row, scale_row):
    ms = jnp.mean(x * x, axis=-1, keepdims=True)
    y = x * lax.rsqrt(ms + EPS) * gain
    return y * (1.0 + mod[scale_row:scale_row + 1, :]) + mod[shift_row:shift_row + 1, :]


def _norm_kernel(x_ref, g_ref, mod_ref, h_ref):
    h_ref[...] = _modulated_norm(x_ref[...], g_ref[...], mod_ref[0], 0, 1).astype(BF16)


def _group_of_tile(i, tiles_per_batch, n_batch):
    return jnp.minimum(i // tiles_per_batch, n_batch)


def _norm_mix(x_all, gain, mod3, tiles_per_batch, n_batch):
    r, d = x_all.shape
    return pl.pallas_call(
        _norm_kernel,
        grid=(r // TM,),
        in_specs=[pl.BlockSpec((TM, d), lambda i: (i, 0)),
                  pl.BlockSpec((1, d), lambda i: (0, 0)),
                  pl.BlockSpec((1, 8, d), lambda i: (_group_of_tile(i, tiles_per_batch, n_batch), 0, 0))],
        out_specs=pl.BlockSpec((TM, d), lambda i: (i, 0)),
        out_shape=jax.ShapeDtypeStruct((r, d), BF16),
        compiler_params=_params(("arbitrary",)),
        name="norm_mix",
    )(x_all, gain.reshape(1, d), mod3)


def _rope_tables(s_len, dh, pad_rows):
    d_ax = dh // 2
    inv = ROPE_THETA ** (-jnp.arange(0, d_ax, 2, dtype=F32) / d_ax)
    t = jnp.arange(s_len, dtype=jnp.int32)
    fr = (t // GRID_W).astype(F32)[:, None] * inv
    fc = (t % GRID_W).astype(F32)[:, None] * inv
    ang = jnp.concatenate([fr, fr, fc, fc], axis=-1)
    cos, sin = jnp.cos(ang), jnp.sin(ang)
    quarter = jnp.arange(dh) // (dh // 4)
    sin_a = jnp.where(quarter % 2 == 0, -sin, 0.0)
    sin_b = jnp.where(quarter % 2 == 1, sin, 0.0)
    reps = LANES // dh

    def finish(tab, ident):
        tab = jnp.tile(tab, (1, reps))
        return jnp.concatenate([tab, jnp.full((pad_rows, LANES), ident, F32)], axis=0)

    return finish(cos, 1.0), finish(sin_a, 0.0), finish(sin_b, 0.0)


def _norm_rope(t, gain, cos, sin_a, sin_b, group, scale):
    sq = t * t
    if group == LANES:
        ms = jnp.mean(sq, axis=-1, keepdims=True)
    else:
        lane = lax.broadcasted_iota(jnp.int32, t.shape, 1)
        low = lane < group
        s_lo = jnp.sum(jnp.where(low, sq, 0.0), axis=-1, keepdims=True)
        s_hi = jnp.sum(jnp.where(low, 0.0, sq), axis=-1, keepdims=True)
        ms = jnp.where(low, s_lo, s_hi) * (1.0 / group)
    y = t * lax.rsqrt(ms + EPS) * gain
    q = group // 4
    r = y * cos + pltpu.roll(y, LANES - q, 1) * sin_a + pltpu.roll(y, q, 1) * sin_b
    if scale != 1.0:
        r = r * scale
    return r


def _kv_kernel(h_ref, w_ref, c128, a128, b128, c64, a64, b64, gka, gkb, gkc,
               ka_ref, kb_ref, kc_ref, vta_ref, vtb_ref, vtc_ref):
    h = h_ref[...]

    def mm(c0):
        return jnp.dot(h, w_ref[:, c0:c0 + 2 * LANES], preferred_element_type=F32)

    def keys(c0, out_ref, o0, gain, cos, sa, sb, group):
        acc = mm(c0)
        for j in range(2):
            blk = acc[:, j * LANES:(j + 1) * LANES]
            out_ref[:, o0 + j * LANES:o0 + (j + 1) * LANES] = _norm_rope(
                blk, gain[...], cos[...], sa[...], sb[...], group, 1.0).astype(BF16)

    def values(c0, out_ref, h0):
        acc = mm(c0)
        for j in range(2):
            out_ref[h0 + j] = acc[:, j * LANES:(j + 1) * LANES].T.astype(BF16)

    c = 0
    keys(c, ka_ref, 0, gka, c128, a128, b128, HEAD_DIM)
    c += A_KV_W
    values(c, vta_ref, 0)
    c += A_KV_W
    for j in range(B_QK_W // (2 * LANES)):
        keys(c, kb_ref, j * 2 * LANES, gkb, c64, a64, b64, B_QK_DIM)
        c += 2 * LANES
    for j in range(B_V_W // (2 * LANES)):
        values(c, vtb_ref, 2 * j)
        c += 2 * LANES
    keys(c, kc_ref, 0, gkc, c128, a128, b128, HEAD_DIM)
    c += C_KV_W
    values(c, vtc_ref, 0)


def _table_specs(pos_tiles, n_lat_tiles):
    spec = pl.BlockSpec((TM, LANES), lambda i: (jnp.where(i < n_lat_tiles, i % pos_tiles, pos_tiles), 0))
    return [spec] * 6


def _kv_proj(h, w_kv, tabs, gka, gkb, gkc, pos_tiles, n_lat_tiles):
    r, d = h.shape
    nt = r // TM
    vec = pl.BlockSpec((1, LANES), lambda i: (0, 0))
    return pl.pallas_call(
        _kv_kernel,
        grid=(nt,),
        in_specs=[pl.BlockSpec((TM, d), lambda i: (i, 0)),
                  pl.BlockSpec((d, KV_W), lambda i: (0, 0))] + _table_specs(pos_tiles, n_lat_tiles) + [vec] * 3,
        out_specs=[pl.BlockSpec((TM, A_KV_W), lambda i: (i, 0)),
                   pl.BlockSpec((TM, B_QK_W), lambda i: (i, 0)),
                   pl.BlockSpec((TM, C_KV_W), lambda i: (i, 0)),
                   pl.BlockSpec((A_KV_HEADS, LANES, TM), lambda i: (0, 0, i)),
                   pl.BlockSpec((B_HEADS, LANES, TM), lambda i: (0, 0, i)),
                   pl.BlockSpec((C_KV_HEADS, LANES, TM), lambda i: (0, 0, i))],
        out_shape=[jax.ShapeDtypeStruct((r, A_KV_W), BF16),
                   jax.ShapeDtypeStruct((r, B_QK_W), BF16),
                   jax.ShapeDtypeStruct((r, C_KV_W), BF16),
                   jax.ShapeDtypeStruct((A_KV_HEADS, LANES, r), BF16),
                   jax.ShapeDtypeStruct((B_HEADS, LANES, r), BF16),
                   jax.ShapeDtypeStruct((C_KV_HEADS, LANES, r), BF16)],
        compiler_params=_params(("arbitrary",)),
        name="kv_proj",
    )(h, w_kv, *tabs, gka, gkb, gkc)


def _q_kernel(h_ref, w_ref, c128, a128, b128, c64, a64, b64, gqa, gqb, gqc, qa_ref, qb_ref, qc_ref):
    h = h_ref[...]

    def queries(c0, out_ref, o0, gain, cos, sa, sb, group):
        acc = jnp.dot(h, w_ref[:, c0:c0 + 2 * LANES], preferred_element_type=F32)
        for j in range(2):
            blk = acc[:, j * LANES:(j + 1) * LANES]
            out_ref[:, o0 + j * LANES:o0 + (j + 1) * LANES] = _norm_rope(
                blk, gain[...], cos[...], sa[...], sb[...], group, group ** -0.5 * LOG2E).astype(BF16)

    c = 0
    for j in range(A_Q_W // (2 * LANES)):
        queries(c, qa_ref, j * 2 * LANES, gqa, c128, a128, b128, HEAD_DIM)
        c += 2 * LANES
    for j in range(B_QK_W // (2 * LANES)):
        queries(c, qb_ref, j * 2 * LANES, gqb, c64, a64, b64, B_QK_DIM)
        c += 2 * LANES
    for j in range(C_Q_W // (2 * LANES)):
        queries(c, qc_ref, j * 2 * LANES, gqc, c128, a128, b128, HEAD_DIM)
        c += 2 * LANES


def _q_proj(h, w_q, tabs, gqa, gqb, gqc, pos_tiles, n_lat_tiles, nt):
    d = h.shape[1]
    rows = nt * TM
    vec = pl.BlockSpec((1, LANES), lambda i: (0, 0))
    return pl.pallas_call(
        _q_kernel,
        grid=(nt,),
        in_specs=[pl.BlockSpec((TM, d), lambda i: (i, 0)),
                  pl.BlockSpec((d, Q_W), lambda i: (0, KV_W // Q_W))] + _table_specs(pos_tiles, n_lat_tiles) + [vec] * 3,
        out_specs=[pl.BlockSpec((TM, A_Q_W), lambda i: (i, 0)),
                   pl.BlockSpec((TM, B_QK_W), lambda i: (i, 0)),
                   pl.BlockSpec((TM, C_Q_W), lambda i: (i, 0))],
        out_shape=[jax.ShapeDtypeStruct((rows, A_Q_W), BF16),
                   jax.ShapeDtypeStruct((rows, B_QK_W), BF16),
                   jax.ShapeDtypeStruct((rows, C_Q_W), BF16)],
        compiler_params=_params(("arbitrary",)),
        name="q_proj",
    )(h, w_q, *tabs, gqa, gqb, gqc)


def _gate_kernel(h_ref, w_ref, g_ref):
    h = h_ref[...]
    for c in range(0, D_MODEL, 2 * LANES):
        z = jnp.dot(h, w_ref[:, c:c + 2 * LANES], preferred_element_type=F32)
        g_ref[:, c:c + 2 * LANES] = jax.nn.sigmoid(z).astype(BF16)


def _gate_proj(h, w_g, nt):
    d = h.shape[1]
    return pl.pallas_call(
        _gate_kernel,
        grid=(3, nt),
        in_specs=[pl.BlockSpec((TM, d), lambda j, i: (i, 0)),
                  pl.BlockSpec((d, D_MODEL), lambda j, i: (0, (KV_W + Q_W) // D_MODEL + j))],
        out_specs=pl.BlockSpec((TM, D_MODEL), lambda j, i: (i, j)),
        out_shape=jax.ShapeDtypeStruct((nt * TM, 3 * D_MODEL), BF16),
        compiler_params=_params(("arbitrary", "arbitrary")),
        name="gate_proj",
    )(h, w_g)


def _attn_kernel(*refs, heads, dual, tq, lat_len, n_lat_steps, window, has_sink, lambda_init):
    it = iter(refs)
    bound_ref = next(it)
    q_ref, klat_ref, vlat_ref, kctx_ref, vctx_ref = next(it), next(it), next(it), next(it), next(it)
    if dual:
        lq1, lk1, lq2, lk2, sub_ref = next(it), next(it), next(it), next(it), next(it)
    if has_sink:
        sink_ref = next(it)
    o_ref = next(it)
    qe_ref, m_ref, l_ref, acc_ref = next(it), next(it), next(it), next(it)

    step = pl.program_id(2)
    is_latent = step < n_lat_steps
    ncol = (2 if dual else heads) * tq
    if dual:
        q = q_ref[...]
        lane = lax.broadcasted_iota(jnp.int32, q.shape, 1)
        zero = jnp.zeros_like(q)
        qe_ref[0:tq, :] = jnp.where(lane < B_QK_DIM, q, zero)
        qe_ref[tq:2 * tq, :] = jnp.where(lane < B_QK_DIM, zero, q)
    else:
        for hh in range(heads):
            qe_ref[hh * tq:(hh + 1) * tq, :] = q_ref[:, hh * LANES:(hh + 1) * LANES]

    bound = bound_ref[0]
    use_bound = bound <= SAFE_LOG2_BOUND

    def scores(k, valid):
        s = lax.dot_general(k, qe_ref[...], (((1,), (1,)), ((), ())), preferred_element_type=F32)
        return s if valid is None else jnp.where(valid, s, NEG_INF)

    def dense_blocks():
        for c in range(lat_len // KC):
            yield klat_ref[c * KC:(c + 1) * KC, :], vlat_ref[0, :, c * KC:(c + 1) * KC], None

    def window_blocks():
        q0 = step * tq
        kk = lax.broadcasted_iota(jnp.int32, (WINDOW, ncol), 0)
        qq = lax.broadcasted_iota(jnp.int32, (WINDOW, ncol), 1) % tq
        rel = kk - qq
        nblk = lat_len // WINDOW
        for j in range(-1, tq // WINDOW + 1):
            blk = q0 // WINDOW + j
            start = pl.multiple_of(jnp.clip(blk, 0, nblk - 1) * WINDOW, WINDOW)
            dist = rel + j * WINDOW
            in_range = jnp.logical_and(blk >= 0, blk < nblk)
            valid = jnp.logical_and(jnp.abs(dist) <= WINDOW, in_range)
            yield klat_ref[pl.ds(start, WINDOW), :], vlat_ref[0, :, pl.ds(start, WINDOW)], valid

    def sink_logits():
        g = pl.program_id(1)
        return jnp.concatenate(
            [jnp.full((1, tq), sink_ref[g * heads + hh], F32) * LOG2E for hh in range(heads)], axis=1)

    def finalize(o):
        if dual:
            s1 = jnp.sum(lq1[...] * lk1[...], axis=-1, keepdims=True)
            s2 = jnp.sum(lq2[...] * lk2[...], axis=-1, keepdims=True)
            lam = jnp.exp(s1) - jnp.exp(s2) + lambda_init
            od = (o[:, 0:tq] - lam * o[:, tq:2 * tq]).T
            ms = jnp.mean(od * od, axis=-1, keepdims=True)
            od = od * lax.rsqrt(ms + EPS) * sub_ref[...] * (1.0 - lambda_init)
            o_ref[...] = od.astype(BF16)
        else:
            for hh in range(heads):
                o_ref[:, hh * LANES:(hh + 1) * LANES] = o[:, hh * tq:(hh + 1) * tq].T.astype(BF16)

    def bounded_pass():
        def contrib(k, vt, valid):
            p = jnp.exp2(scores(k, valid) - bound)
            return jnp.dot(vt, p.astype(BF16), preferred_element_type=F32), jnp.sum(p, axis=0, keepdims=True)

        def latent():
            acc = den = None
            for k, vt, valid in (window_blocks() if window else dense_blocks()):
                da, dl = contrib(k, vt, valid)
                acc = da if acc is None else acc + da
                den = dl if den is None else den + dl
            return acc, den

        def nothing():
            return jnp.zeros((LANES, ncol), F32), jnp.zeros((1, ncol), F32)

        acc_c, den_c = contrib(kctx_ref[...], vctx_ref[0], None)
        acc_l, den_l = lax.cond(is_latent, latent, nothing)
        den = den_c + den_l
        if has_sink:
            den = den + jnp.exp2(sink_logits() - bound)
        finalize((acc_c + acc_l) * (1.0 / den))

    def online_pass():
        if has_sink:
            m_ref[...] = sink_logits()
            l_ref[...] = jnp.ones((1, ncol), F32)
        else:
            m_ref[...] = jnp.full((1, ncol), NEG_INF, F32)
            l_ref[...] = jnp.zeros((1, ncol), F32)
        acc_ref[...] = jnp.zeros((LANES, ncol), F32)

        def chunk(k, vt, valid=None):
            s = scores(k, valid)
            m_old = m_ref[...]
            m_new = jnp.maximum(m_old, jnp.max(s, axis=0, keepdims=True))
            alpha = jnp.exp2(m_old - m_new)
            p = jnp.exp2(s - m_new)
            l_ref[...] = alpha * l_ref[...] + jnp.sum(p, axis=0, keepdims=True)
            acc_ref[...] = alpha * acc_ref[...] + jnp.dot(vt, p.astype(BF16), preferred_element_type=F32)
            m_ref[...] = m_new

        chunk(kctx_ref[...], vctx_ref[0])

        @pl.when(is_latent)
        def _():
            if window:
                for k, vt, valid in window_blocks():
                    chunk(k, vt, valid)
            else:
                def body(c, carry):
                    start = pl.multiple_of(c * KC, KC)
                    chunk(klat_ref[pl.ds(start, KC), :], vlat_ref[0, :, pl.ds(start, KC)])
                    return carry
                lax.fori_loop(0, lat_len // KC, body, 0)

        finalize(acc_ref[...] * (1.0 / l_ref[...]))

    pl.when(use_bound)(bounded_pass)
    pl.when(jnp.logical_not(use_bound))(online_pass)


def _score_bound(q_gain, k_gain, group, sink=None):
    bound = jnp.max(jnp.abs(q_gain)) * jnp.max(jnp.abs(k_gain)) * (math.sqrt(group) * LOG2E * BOUND_MARGIN)
    if sink is not None:
        bound = jnp.maximum(bound, jnp.max(sink) * LOG2E)
    return bound.reshape(1).astype(F32)


def _attention(bound, q, k_all, vt_all, *, n_batch, s_len, ctx_len, ctx_queries, heads, n_groups, dual, window,
               lambda_init=0.0, lam_params=None, subln=None, sink=None):
    assert ctx_len == TQ
    width = heads * LANES
    nq = s_len // TQ
    ctx_blk0 = n_batch * s_len // ctx_len
    ncol = (2 if dual else heads) * TQ
    qmap = lambda b, g, i: (jnp.where(i < nq, b * nq + i, ctx_blk0 + b), g)

    in_specs = [pl.BlockSpec(memory_space=pltpu.SMEM),
                pl.BlockSpec((TQ, width), qmap),
                pl.BlockSpec((s_len, LANES), lambda b, g, i: (b, g)),
                pl.BlockSpec((1, LANES, s_len), lambda b, g, i: (g, 0, b)),
                pl.BlockSpec((ctx_len, LANES), lambda b, g, i: (ctx_blk0 + b, g)),
                pl.BlockSpec((1, LANES, ctx_len), lambda b, g, i: (g, 0, ctx_blk0 + b))]
    args = [bound, q, k_all, vt_all, k_all, vt_all]
    if dual:
        in_specs += [pl.BlockSpec((1, B_QK_DIM), lambda b, g, i: (0, 0))] * 4
        in_specs += [pl.BlockSpec((1, LANES), lambda b, g, i: (0, 0))]
        args += [p.reshape(1, B_QK_DIM) for p in lam_params] + [subln.reshape(1, LANES)]
    if sink is not None:
        in_specs += [pl.BlockSpec(memory_space=pltpu.SMEM)]
        args += [sink]

    kern = functools.partial(_attn_kernel, heads=heads, dual=dual, tq=TQ, lat_len=s_len, n_lat_steps=nq,
                             window=window, has_sink=sink is not None, lambda_init=lambda_init)
    return pl.pallas_call(
        kern,
        grid=(n_batch, n_groups, nq + (1 if ctx_queries else 0)),
        in_specs=in_specs,
        out_specs=pl.BlockSpec((TQ, width), qmap),
        out_shape=jax.ShapeDtypeStruct((q.shape[0], n_groups * width), BF16),
        scratch_shapes=[pltpu.VMEM((ncol, LANES), BF16), pltpu.VMEM((1, ncol), F32),
                        pltpu.VMEM((1, ncol), F32), pltpu.VMEM((LANES, ncol), F32)],
        compiler_params=_params(("arbitrary", "arbitrary", "arbitrary")),
        name="attention",
    )(*args)


def _merge_kernel(oa_ref, ob_ref, oc_ref, g_ref, wa_ref, wb_ref, wc_ref, m_ref):
    oa, ob, oc = oa_ref[...], ob_ref[...], oc_ref[...]
    step = 4 * LANES
    for c in range(0, D_MODEL, step):
        sl = slice(c, c + step)
        m = g_ref[:, c:c + step].astype(F32) * jnp.dot(oa, wa_ref[:, sl], preferred_element_type=F32)
        m = m + g_ref[:, D_MODEL + c:D_MODEL + c + step].astype(F32) * jnp.dot(
            ob, wb_ref[:, sl], preferred_element_type=F32)
        m = m + g_ref[:, 2 * D_MODEL + c:2 * D_MODEL + c + step].astype(F32) * jnp.dot(
            oc, wc_ref[:, sl], preferred_element_type=F32)
        m_ref[:, sl] = m.astype(BF16)


def _merge(oa, ob, oc, gates, wa, wb, wc, nt):
    const = lambda i: (0, 0)
    return pl.pallas_call(
        _merge_kernel,
        grid=(nt,),
        in_specs=[pl.BlockSpec((TM, A_Q_W), lambda i: (i, 0)),
                  pl.BlockSpec((TM, B_V_W), lambda i: (i, 0)),
                  pl.BlockSpec((TM, C_Q_W), lambda i: (i, 0)),
                  pl.BlockSpec((TM, 3 * D_MODEL), lambda i: (i, 0)),
                  pl.BlockSpec((A_Q_W, D_MODEL), const),
                  pl.BlockSpec((B_V_W, D_MODEL), const),
                  pl.BlockSpec((C_Q_W, D_MODEL), const)],
        out_specs=pl.BlockSpec((TM, D_MODEL), lambda i: (i, 0)),
        out_shape=jax.ShapeDtypeStruct((nt * TM, D_MODEL), BF16),
        compiler_params=_params(("arbitrary",)),
        name="merge",
    )(oa, ob, oc, gates, wa, wb, wc)


def _out_kernel(m_ref, w_ref, x_ref, mod_ref, o_ref):
    m = m_ref[...]
    step = 4 * LANES
    for c in range(0, D_MODEL, step):
        o = jnp.dot(m, w_ref[:, c:c + step], preferred_element_type=F32)
        o_ref[:, c:c + step] = x_ref[:, c:c + step] + mod_ref[0, 2:3, c:c + step] * o


def _out_proj(m, w_out, x_all, mod3, nt, tiles_per_batch, n_batch):
    return pl.pallas_call(
        _out_kernel,
        grid=(nt,),
        in_specs=[pl.BlockSpec((TM, D_MODEL), lambda i: (i, 0)),
                  pl.BlockSpec((D_MODEL, D_MODEL), lambda i: (0, 0)),
                  pl.BlockSpec((TM, D_MODEL), lambda i: (i, 0)),
                  pl.BlockSpec((1, 8, D_MODEL), lambda i: (_group_of_tile(i, tiles_per_batch, n_batch), 0, 0))],
        out_specs=pl.BlockSpec((TM, D_MODEL), lambda i: (i, 0)),
        out_shape=jax.ShapeDtypeStruct((nt * TM, D_MODEL), F32),
        compiler_params=_params(("arbitrary",)),
        name="out_proj",
    )(m, w_out, x_all, mod3)


def _norm_router_kernel(x_ref, g_ref, mod_ref, wr_ref, h_ref, aff_ref):
    h = _modulated_norm(x_ref[...], g_ref[...], mod_ref[0], 3, 4)
    h_ref[...] = h
    logits = _dot3(wr_ref[...], h, ((1,), (1,)))
    e = jnp.exp(logits - jnp.max(logits, axis=0, keepdims=True))
    aff_ref[...] = e / jnp.sum(e, axis=0, keepdims=True)


def _norm_router(x_all, gain, mod3, w_router_t, nt, tiles_per_batch, n_batch):
    d = x_all.shape[1]
    return pl.pallas_call(
        _norm_router_kernel,
        grid=(nt,),
        in_specs=[pl.BlockSpec((TM, d), lambda i: (i, 0)),
                  pl.BlockSpec((1, d), lambda i: (0, 0)),
                  pl.BlockSpec((1, 8, d), lambda i: (_group_of_tile(i, tiles_per_batch, n_batch), 0, 0)),
                  pl.BlockSpec((N_EXPERTS, d), lambda i: (0, 0))],
        out_specs=[pl.BlockSpec((TM, d), lambda i: (i, 0)),
                   pl.BlockSpec((N_EXPERTS, TM), lambda i: (0, i))],
        out_shape=[jax.ShapeDtypeStruct((nt * TM, d), F32),
                   jax.ShapeDtypeStruct((N_EXPERTS, nt * TM), F32)],
        compiler_params=_params(("arbitrary",)),
        name="norm_router",
    )(x_all, gain.reshape(1, d), mod3, w_router_t)


def _prefix_count(mask, n):
    ii = lax.broadcasted_iota(jnp.int32, (LANES, LANES), 0)
    jj = lax.broadcasted_iota(jnp.int32, (LANES, LANES), 1)
    tri = jnp.where(ii <= jj, 1.0, 0.0).astype(BF16)
    m = jnp.where(mask, 1.0, 0.0).astype(BF16)
    outs = []
    off = jnp.zeros((mask.shape[0], 1), F32)
    for b in range(n // LANES):
        c = jnp.dot(m[:, b * LANES:(b + 1) * LANES], tri, preferred_element_type=F32) + off
        outs.append(c)
        off = c[:, LANES - 1:LANES]
    return jnp.concatenate(outs, axis=1)


def _route_kernel(aff_ref, idx_ref, gate_ref, csel_ref, *, n, cap, row_base):
    a = aff_ref[...]
    bits = pltpu.bitcast(a, jnp.int32)
    kf = float(cap)

    def search(i, t):
        cand = t | jnp.left_shift(jnp.int32(1), 30 - i)
        cnt = jnp.sum(jnp.where(bits >= cand, 1.0, 0.0), axis=1, keepdims=True)
        return jnp.where(cnt >= kf, cand, t)

    thr = lax.fori_loop(0, 31, search, jnp.zeros((N_EXPERTS, 1), jnp.int32))
    above = bits > thr
    equal = bits == thr
    need = kf - jnp.sum(jnp.where(above, 1.0, 0.0), axis=1, keepdims=True)
    chosen = jnp.logical_or(above, jnp.logical_and(equal, _prefix_count(equal, n) <= need))
    csel_ref[...] = jnp.where(chosen, _prefix_count(chosen, n), 0.0)

    tok = lax.broadcasted_iota(jnp.int32, (1, n), 1).astype(F32)
    sub = min(cap, 64)
    base = row_base + pl.program_id(0) * n

    def per_expert(e, carry):
        ce = csel_ref[pl.ds(e, 1), :]
        ae = aff_ref[pl.ds(e, 1), :]
        for s0 in range(0, cap, sub):
            slot = (lax.broadcasted_iota(jnp.int32, (sub, n), 0) + (s0 + 1)).astype(F32)
            hit = ce == slot
            tsel = jnp.sum(jnp.where(hit, tok, 0.0), axis=1, keepdims=True)
            gsel = jnp.sum(jnp.where(hit, ae, 0.0), axis=1, keepdims=True)
            idx_ref[0, e, s0:s0 + sub, :] = tsel.astype(jnp.int32) + base
            gate_ref[0, e, s0:s0 + sub, :] = gsel
        return carry

    lax.fori_loop(0, N_EXPERTS, per_expert, 0)


def _route(aff_t, n_sets, n, cap, first_block, row_base):
    return pl.pallas_call(
        functools.partial(_route_kernel, n=n, cap=cap, row_base=row_base),
        grid=(n_sets,),
        in_specs=[pl.BlockSpec((N_EXPERTS, n), lambda s: (0, first_block + s))],
        out_specs=[pl.BlockSpec((1, N_EXPERTS, cap, 1), lambda s: (s, 0, 0, 0)),
                   pl.BlockSpec((1, N_EXPERTS, cap, 1), lambda s: (s, 0, 0, 0))],
        out_shape=[jax.ShapeDtypeStruct((n_sets, N_EXPERTS, cap, 1), jnp.int32),
                   jax.ShapeDtypeStruct((n_sets, N_EXPERTS, cap, 1), F32)],
        scratch_shapes=[pltpu.VMEM((N_EXPERTS, n), F32)],
        compiler_params=_params(("arbitrary",)),
        name="route",
    )(aff_t)


def _ffn_kernel(rows_ref, h_hbm, wg_ref, wu_ref, wd_ref, gate_ref, mod_ref, x_in_hbm, x_hbm,
                buf, xg, yacc, sem, *, nr, n_ff, groups):
    del x_in_hbm
    e = pl.program_id(0)
    f = pl.program_id(1)
    base = e * nr

    def row_copies(hbm, to_hbm):
        def copy(hbm_rows, buf_rows):
            if to_hbm:
                return pltpu.make_async_copy(buf_rows, hbm_rows, sem)
            return pltpu.make_async_copy(hbm_rows, buf_rows, sem)

        def issue(i, carry):
            r = rows_ref[base + i]
            copy(hbm.at[pl.ds(r, 1), :], buf.at[pl.ds(i, 1), :]).start()
            return carry

        lax.fori_loop(0, nr, issue, 0, unroll=8)
        copy(hbm.at[pl.ds(0, nr), :], buf).wait()

    @pl.when(f == 0)
    def _():
        row_copies(h_hbm, False)
        xg[...] = buf[...].astype(BF16)

    x = xg[...]
    u = jnp.dot(x, wg_ref[0].astype(BF16), preferred_element_type=F32)
    v = jnp.dot(x, wu_ref[0].astype(BF16), preferred_element_type=F32)
    act = (u * jax.nn.sigmoid(u) * v).astype(BF16)
    contrib = jnp.dot(act, wd_ref[0].astype(BF16), preferred_element_type=F32)

    @pl.when(f == 0)
    def _():
        yacc[...] = contrib

    @pl.when(f > 0)
    def _():
        yacc[...] += contrib

    @pl.when(f == n_ff - 1)
    def _():
        row_copies(x_hbm, False)
        for (r0, r1, grp) in groups:
            buf[r0:r1, :] += yacc[r0:r1, :] * gate_ref[0, r0:r1, :] * mod_ref[grp, 5:6, :]
        row_copies(x_hbm, True)


def _expert_ffn(rows, h2, w_gate, w_up, w_down, gates, mod3, x_all, nr, groups):
    d = h2.shape[1]
    n_ff = EXPERT_FF // FF_TILE
    kern = functools.partial(_ffn_kernel, nr=nr, n_ff=n_ff, groups=groups)
    return pl.pallas_call(
        kern,
        grid_spec=pltpu.PrefetchScalarGridSpec(
            num_scalar_prefetch=1,
            grid=(N_EXPERTS, n_ff),
            in_specs=[pl.BlockSpec(memory_space=pl.ANY),
                      pl.BlockSpec((1, d, FF_TILE), lambda e, f, rows: (e, 0, f)),
                      pl.BlockSpec((1, d, FF_TILE), lambda e, f, rows: (e, 0, f)),
                      pl.BlockSpec((1, FF_TILE, d), lambda e, f, rows: (e, f, 0)),
                      pl.BlockSpec((1, nr, 1), lambda e, f, rows: (e, 0, 0)),
                      pl.BlockSpec((3, 8, d), lambda e, f, rows: (0, 0, 0)),
                      pl.BlockSpec(memory_space=pl.ANY)],
            out_specs=pl.BlockSpec(memory_space=pl.ANY),
            scratch_shapes=[pltpu.VMEM((nr, d), F32), pltpu.VMEM((nr, d), BF16), pltpu.VMEM((nr, d), F32),
                            pltpu.SemaphoreType.DMA(())]),
        out_shape=jax.ShapeDtypeStruct(x_all.shape, F32),
        input_output_aliases={7: 0},
        compiler_params=_params(("arbitrary", "arbitrary")),
        name="expert_ffn",
    )(rows, h2, w_gate, w_up, w_down, gates, mod3, x_all)


def kernel(x, c, ctx, c_ctx, w_mod, b_mod, norm_mix, norm_ffn, w_in, qn_a, kn_a, qn_b, kn_b,
           lam_q1, lam_k1, lam_q2, lam_k2, subln_b, qn_c, kn_c, sink_c,
           w_br_a, w_br_b, w_br_c, w_out, w_router, w_gate, w_up, w_down):
    n_batch, s_len, d = x.shape
    ctx_len = ctx.shape[1]
    depth = w_mod.shape[0]
    lat_rows = n_batch * s_len
    tiles_per_batch = s_len // TM
    n_lat_tiles = lat_rows // TM
    assert s_len % TM == 0 and (n_batch * ctx_len) == TM and s_len % KC == 0 and d == D_MODEL
    assert KV_W == Q_W == D_MODEL

    cvec = jnp.zeros((8, d), F32).at[:n_batch].set(c).at[n_batch].set(c_ctx)
    mod_all = _modulation(cvec, w_mod, b_mod)

    tabs = _rope_tables(s_len, HEAD_DIM, TM) + _rope_tables(s_len, B_QK_DIM, TM)
    x_all = jnp.concatenate([x.reshape(lat_rows, d), ctx.reshape(n_batch * ctx_len, d)], axis=0)

    cap_lat = EC_CAPACITY * s_len // N_EXPERTS
    cap_ctx = EC_CAPACITY * ctx_len // N_EXPERTS
    lane_vec = lambda v: v.reshape(1, -1) if v.shape[-1] == LANES else jnp.tile(v, LANES // v.shape[-1]).reshape(1, LANES)

    for l in range(depth):
        last = l == depth - 1
        lambda_init = 0.8 - 0.6 * math.exp(-0.3 * l)
        nt = n_lat_tiles if last else n_lat_tiles + 1
        mod3 = jnp.pad(mod_all[l, :n_batch + 1].reshape(n_batch + 1, 6, d), ((0, 0), (0, 2), (0, 0)))

        h = _norm_mix(x_all, norm_mix[l], mod3, tiles_per_batch, n_batch)
        w_in_l = w_in[l].astype(BF16)
        ka, kb, kc, vta, vtb, vtc = _kv_proj(h, w_in_l, tabs, lane_vec(kn_a[l]), lane_vec(kn_b[l]),
                                             lane_vec(kn_c[l]), tiles_per_batch, n_lat_tiles)
        qa, qb, qc = _q_proj(h, w_in_l, tabs, lane_vec(qn_a[l]), lane_vec(qn_b[l]),
                             lane_vec(qn_c[l]), tiles_per_batch, n_lat_tiles, nt)
        gates = _gate_proj(h, w_in_l, nt)

        common = dict(n_batch=n_batch, s_len=s_len, ctx_len=ctx_len, ctx_queries=not last)
        lam_params = (lam_q1[l], lam_k1[l], lam_q2[l], lam_k2[l])
        oa = _attention(_score_bound(qn_a[l], kn_a[l], HEAD_DIM), qa, ka, vta, heads=A_Q_HEADS // A_KV_HEADS,
                        n_groups=A_KV_HEADS, dual=False, window=False, **common)
        ob = _attention(_score_bound(qn_b[l], kn_b[l], B_QK_DIM), qb, kb, vtb, heads=1, n_groups=B_HEADS, dual=True,
                        window=False, lambda_init=lambda_init, lam_params=lam_params, subln=subln_b[l], **common)
        oc = _attention(_score_bound(qn_c[l], kn_c[l], HEAD_DIM, sink_c[l]), qc, kc, vtc,
                        heads=C_Q_HEADS // C_KV_HEADS, n_groups=C_KV_HEADS, dual=False, window=True,
                        sink=sink_c[l], **common)

        m = _merge(oa, ob, oc, gates, w_br_a[l].astype(BF16), w_br_b[l].astype(BF16), w_br_c[l].astype(BF16), nt)
        x_mid = _out_proj(m, w_out[l].astype(BF16), x_all, mod3, nt, tiles_per_batch, n_batch)

        h2, aff_t = _norm_router(x_mid, norm_ffn[l], mod3, w_router[l].T, nt, tiles_per_batch, n_batch)
        idx_lat, gate_lat = _route(aff_t, n_batch, s_len, cap_lat, 0, 0)
        idx_parts = [idx_lat[b, :, :, 0] for b in range(n_batch)]
        gate_parts = [gate_lat[b] for b in range(n_batch)]
        groups = [(b * cap_lat, (b + 1) * cap_lat, b) for b in range(n_batch)]
        if not last:
            idx_ctx, gate_ctx = _route(aff_t, n_batch, ctx_len, cap_ctx, lat_rows // ctx_len, lat_rows)
            idx_parts += [idx_ctx[b, :, :, 0] for b in range(n_batch)]
            gate_parts += [gate_ctx[b] for b in range(n_batch)]
            groups.append((n_batch * cap_lat, n_batch * (cap_lat + cap_ctx), n_batch))
        rows = jnp.concatenate(idx_parts, axis=1)
        gate_rows = jnp.concatenate(gate_parts, axis=1)
        nr = rows.shape[1]
        x_all = _expert_ffn(rows.reshape(-1), h2, w_gate[l], w_up[l], w_down[l], gate_rows, mod3, x_mid,
                            nr, tuple(groups))

    return x_all[:lat_rows].reshape(n_batch, s_len, d)
```

```python
import functools
import math

import jax
import jax.numpy as jnp
from jax import lax
from jax.experimental import pallas as pl
from jax.experimental.pallas import tpu as pltpu

F32 = jnp.float32
BF16 = jnp.bfloat16

D_MODEL = 2048
GRID_W = 64
HEAD_DIM = 128
A_Q_HEADS, A_KV_HEADS = 6, 2
B_HEADS, B_QK_DIM = 4, 64
C_Q_HEADS, C_KV_HEADS = 6, 2
WINDOW = 128
N_EXPERTS = 16
EXPERT_FF = 1024
EC_CAPACITY = 2
ROPE_THETA = 10000.0
EPS = 1e-6
NEG_INF = -1e30

A_Q_W = A_Q_HEADS * HEAD_DIM
A_KV_W = A_KV_HEADS * HEAD_DIM
B_QK_W = B_HEADS * 2 * B_QK_DIM
B_V_W = B_HEADS * HEAD_DIM
C_Q_W = C_Q_HEADS * HEAD_DIM
C_KV_W = C_KV_HEADS * HEAD_DIM
KV_W = 2 * A_KV_W + B_QK_W + B_V_W + 2 * C_KV_W
Q_W = A_Q_W + B_QK_W + C_Q_W

LANES = 128
TM = 512
TQ = 256
KC = 512
FF_TILE = 256
TT = 256
WIN = 64
ROW_ALIGN = 16
VMEM_LIMIT = 56 * 1024 * 1024
LOG2E = 1.4426950408889634
SAFE_LOG2_BOUND = 60.0
BOUND_MARGIN = 1.03


def _params(sem, vmem=VMEM_LIMIT):
    return pltpu.CompilerParams(dimension_semantics=sem, vmem_limit_bytes=vmem)


def _split_bf16(x):
    hi = x.astype(BF16)
    lo = (x - hi.astype(F32)).astype(BF16)
    return hi, lo


def _dot3(a, b, dims):
    ah, al = _split_bf16(a)
    bh, bl = _split_bf16(b)
    dg = lambda x, y: lax.dot_general(x, y, (dims, ((), ())), preferred_element_type=F32)
    return dg(ah, bh) + (dg(ah, bl) + dg(al, bh))


def _resident_weight_spec(block, index_map):
    return pl.BlockSpec(block, index_map, pipeline_mode=pl.Buffered(1))


def _cast_weight_once(w_ref, wbf_ref, first):
    @pl.when(first)
    def _():
        wbf_ref[...] = w_ref[0].astype(BF16)


def _mod_kernel(c_ref, w_ref, b_ref, o_ref):
    c = c_ref[...]
    a = c * jax.nn.sigmoid(c)
    o_ref[0] = _dot3(a, w_ref[0], ((1,), (0,))) + b_ref[0]


def _modulation(cvec, w_mod, b_mod):
    depth, d, n = w_mod.shape
    tn = 1024
    return pl.pallas_call(
        _mod_kernel,
        grid=(depth, n // tn),
        in_specs=[pl.BlockSpec((8, d), lambda l, j: (0, 0)),
                  pl.BlockSpec((1, d, tn), lambda l, j: (l, 0, j)),
                  pl.BlockSpec((1, 1, tn), lambda l, j: (l, 0, j))],
        out_specs=pl.BlockSpec((1, 8, tn), lambda l, j: (l, 0, j)),
        out_shape=jax.ShapeDtypeStruct((depth, 8, n), F32),
        compiler_params=_params(("arbitrary", "arbitrary")),
        name="modulation",
    )(cvec, w_mod, b_mod.reshape(depth, 1, n))


def _modulated_norm(x, gain, mod, shift_row, scale_row):
    ms = jnp.mean(x * x, axis=-1, keepdims=True)
    y = x * lax.rsqrt(ms + EPS) * gain
    return y * (1.0 + mod[scale_row:scale_row + 1, :]) + mod[shift_row:shift_row + 1, :]


def _norm_kernel(x_ref, g_ref, mod_ref, h_ref):
    h_ref[...] = _modulated_norm(x_ref[...], g_ref[...], mod_ref[0], 0, 1).astype(BF16)


def _group_of_tile(i, tiles_per_batch, n_batch):
    return jnp.minimum(i // tiles_per_batch, n_batch)


def _norm_mix(x_all, gain, mod3, tiles_per_batch, n_batch):
    r, d = x_all.shape
    return pl.pallas_call(
        _norm_kernel,
        grid=(r // TM,),
        in_specs=[pl.BlockSpec((TM, d), lambda i: (i, 0)),
                  pl.BlockSpec((1, d), lambda i: (0, 0)),
                  pl.BlockSpec((1, 8, d), lambda i: (_group_of_tile(i, tiles_per_batch, n_batch), 0, 0))],
        out_specs=pl.BlockSpec((TM, d), lambda i: (i, 0)),
        out_shape=jax.ShapeDtypeStruct((r, d), BF16),
        compiler_params=_params(("arbitrary",)),
        name="norm_mix",
    )(x_all, gain.reshape(1, d), mod3)


def _rope_tables(s_len, dh, pad_rows):
    d_ax = dh // 2
    inv = ROPE_THETA ** (-jnp.arange(0, d_ax, 2, dtype=F32) / d_ax)
    t = jnp.arange(s_len, dtype=jnp.int32)
    fr = (t // GRID_W).astype(F32)[:, None] * inv
    fc = (t % GRID_W).astype(F32)[:, None] * inv
    ang = jnp.concatenate([fr, fr, fc, fc], axis=-1)
    cos, sin = jnp.cos(ang), jnp.sin(ang)
    quarter = jnp.arange(dh) // (dh // 4)
    sin_a = jnp.where(quarter % 2 == 0, -sin, 0.0)
    sin_b = jnp.where(quarter % 2 == 1, sin, 0.0)
    reps = LANES // dh

    def finish(tab, ident):
        tab = jnp.tile(tab, (1, reps))
        return jnp.concatenate([tab, jnp.full((pad_rows, LANES), ident, F32)], axis=0)

    return finish(cos, 1.0), finish(sin_a, 0.0), finish(sin_b, 0.0)


def _norm_rope(t, gain, cos, sin_a, sin_b, group, scale):
    sq = t * t
    if group == LANES:
        ms = jnp.mean(sq, axis=-1, keepdims=True)
    else:
        lane = lax.broadcasted_iota(jnp.int32, t.shape, 1)
        low = lane < group
        s_lo = jnp.sum(jnp.where(low, sq, 0.0), axis=-1, keepdims=True)
        s_hi = jnp.sum(jnp.where(low, 0.0, sq), axis=-1, keepdims=True)
        ms = jnp.where(low, s_lo, s_hi) * (1.0 / group)
    y = t * lax.rsqrt(ms + EPS) * gain
    q = group // 4
    r = y * cos + pltpu.roll(y, LANES - q, 1) * sin_a + pltpu.roll(y, q, 1) * sin_b
    if scale != 1.0:
        r = r * scale
    return r


def _kv_kernel(h_ref, w32_ref, c128, a128, b128, c64, a64, b64, gka, gkb, gkc,
               ka_ref, kb_ref, kc_ref, vta_ref, vtb_ref, vtc_ref, w_ref):
    _cast_weight_once(w32_ref, w_ref, pl.program_id(0) == 0)
    h = h_ref[...]

    def mm(c0):
        return jnp.dot(h, w_ref[:, c0:c0 + 2 * LANES], preferred_element_type=F32)

    def keys(c0, out_ref, o0, gain, cos, sa, sb, group):
        acc = mm(c0)
        for j in range(2):
            blk = acc[:, j * LANES:(j + 1) * LANES]
            out_ref[:, o0 + j * LANES:o0 + (j + 1) * LANES] = _norm_rope(
                blk, gain[...], cos[...], sa[...], sb[...], group, 1.0).astype(BF16)

    def values(c0, out_ref, h0):
        acc = mm(c0)
        for j in range(2):
            out_ref[h0 + j] = acc[:, j * LANES:(j + 1) * LANES].T.astype(BF16)

    c = 0
    keys(c, ka_ref, 0, gka, c128, a128, b128, HEAD_DIM)
    c += A_KV_W
    values(c, vta_ref, 0)
    c += A_KV_W
    for j in range(B_QK_W // (2 * LANES)):
        keys(c, kb_ref, j * 2 * LANES, gkb, c64, a64, b64, B_QK_DIM)
        c += 2 * LANES
    for j in range(B_V_W // (2 * LANES)):
        values(c, vtb_ref, 2 * j)
        c += 2 * LANES
    keys(c, kc_ref, 0, gkc, c128, a128, b128, HEAD_DIM)
    c += C_KV_W
    values(c, vtc_ref, 0)


def _table_specs(pos_tiles, n_lat_tiles):
    spec = pl.BlockSpec((TM, LANES), lambda i: (jnp.where(i < n_lat_tiles, i % pos_tiles, pos_tiles), 0))
    return [spec] * 6


def _kv_proj(layer, h, w_in, tabs, gka, gkb, gkc, pos_tiles, n_lat_tiles):
    r, d = h.shape
    nt = r // TM
    vec = pl.BlockSpec((1, LANES), lambda i: (0, 0))
    return pl.pallas_call(
        _kv_kernel,
        grid=(nt,),
        in_specs=[pl.BlockSpec((TM, d), lambda i: (i, 0)),
                  _resident_weight_spec((1, d, KV_W), lambda i: (layer, 0, 0))]
        + _table_specs(pos_tiles, n_lat_tiles) + [vec] * 3,
        out_specs=[pl.BlockSpec((TM, A_KV_W), lambda i: (i, 0)),
                   pl.BlockSpec((TM, B_QK_W), lambda i: (i, 0)),
                   pl.BlockSpec((TM, C_KV_W), lambda i: (i, 0)),
                   pl.BlockSpec((A_KV_HEADS, LANES, TM), lambda i: (0, 0, i)),
                   pl.BlockSpec((B_HEADS, LANES, TM), lambda i: (0, 0, i)),
                   pl.BlockSpec((C_KV_HEADS, LANES, TM), lambda i: (0, 0, i))],
        out_shape=[jax.ShapeDtypeStruct((r, A_KV_W), BF16),
                   jax.ShapeDtypeStruct((r, B_QK_W), BF16),
                   jax.ShapeDtypeStruct((r, C_KV_W), BF16),
                   jax.ShapeDtypeStruct((A_KV_HEADS, LANES, r), BF16),
                   jax.ShapeDtypeStruct((B_HEADS, LANES, r), BF16),
                   jax.ShapeDtypeStruct((C_KV_HEADS, LANES, r), BF16)],
        scratch_shapes=[pltpu.VMEM((d, KV_W), BF16)],
        compiler_params=_params(("arbitrary",)),
        name="kv_proj",
    )(h, w_in, *tabs, gka, gkb, gkc)


def _q_kernel(h_ref, w32_ref, c128, a128, b128, c64, a64, b64, gqa, gqb, gqc, qa_ref, qb_ref, qc_ref, w_ref):
    _cast_weight_once(w32_ref, w_ref, pl.program_id(0) == 0)
    h = h_ref[...]

    def queries(c0, out_ref, o0, gain, cos, sa, sb, group):
        acc = jnp.dot(h, w_ref[:, c0:c0 + 2 * LANES], preferred_element_type=F32)
        for j in range(2):
            blk = acc[:, j * LANES:(j + 1) * LANES]
            out_ref[:, o0 + j * LANES:o0 + (j + 1) * LANES] = _norm_rope(
                blk, gain[...], cos[...], sa[...], sb[...], group, group ** -0.5 * LOG2E).astype(BF16)

    c = 0
    for j in range(A_Q_W // (2 * LANES)):
        queries(c, qa_ref, j * 2 * LANES, gqa, c128, a128, b128, HEAD_DIM)
        c += 2 * LANES
    for j in range(B_QK_W // (2 * LANES)):
        queries(c, qb_ref, j * 2 * LANES, gqb, c64, a64, b64, B_QK_DIM)
        c += 2 * LANES
    for j in range(C_Q_W // (2 * LANES)):
        queries(c, qc_ref, j * 2 * LANES, gqc, c128, a128, b128, HEAD_DIM)
        c += 2 * LANES


def _q_proj(layer, h, w_in, tabs, gqa, gqb, gqc, pos_tiles, n_lat_tiles, nt):
    d = h.shape[1]
    rows = nt * TM
    vec = pl.BlockSpec((1, LANES), lambda i: (0, 0))
    return pl.pallas_call(
        _q_kernel,
        grid=(nt,),
        in_specs=[pl.BlockSpec((TM, d), lambda i: (i, 0)),
                  _resident_weight_spec((1, d, Q_W), lambda i: (layer, 0, KV_W // Q_W))]
        + _table_specs(pos_tiles, n_lat_tiles) + [vec] * 3,
        out_specs=[pl.BlockSpec((TM, A_Q_W), lambda i: (i, 0)),
                   pl.BlockSpec((TM, B_QK_W), lambda i: (i, 0)),
                   pl.BlockSpec((TM, C_Q_W), lambda i: (i, 0))],
        out_shape=[jax.ShapeDtypeStruct((rows, A_Q_W), BF16),
                   jax.ShapeDtypeStruct((rows, B_QK_W), BF16),
                   jax.ShapeDtypeStruct((rows, C_Q_W), BF16)],
        scratch_shapes=[pltpu.VMEM((d, Q_W), BF16)],
        compiler_params=_params(("arbitrary",)),
        name="q_proj",
    )(h, w_in, *tabs, gqa, gqb, gqc)


def _gate_kernel(h_ref, w32_ref, g_ref, w_ref):
    _cast_weight_once(w32_ref, w_ref, pl.program_id(1) == 0)
    h = h_ref[...]
    for c in range(0, D_MODEL, 2 * LANES):
        z = jnp.dot(h, w_ref[:, c:c + 2 * LANES], preferred_element_type=F32)
        g_ref[:, c:c + 2 * LANES] = jax.nn.sigmoid(z).astype(BF16)


def _gate_proj(layer, h, w_in, nt):
    d = h.shape[1]
    return pl.pallas_call(
        _gate_kernel,
        grid=(3, nt),
        in_specs=[pl.BlockSpec((TM, d), lambda j, i: (i, 0)),
                  _resident_weight_spec((1, d, D_MODEL), lambda j, i: (layer, 0, (KV_W + Q_W) // D_MODEL + j))],
        out_specs=pl.BlockSpec((TM, D_MODEL), lambda j, i: (i, j)),
        out_shape=jax.ShapeDtypeStruct((nt * TM, 3 * D_MODEL), BF16),
        scratch_shapes=[pltpu.VMEM((d, D_MODEL), BF16)],
        compiler_params=_params(("arbitrary", "arbitrary")),
        name="gate_proj",
    )(h, w_in)


def _attn_kernel(*refs, heads, dual, tq, lat_len, n_lat_steps, window, has_sink, lambda_init):
    it = iter(refs)
    bound_ref = next(it)
    q_ref, klat_ref, vlat_ref, kctx_ref, vctx_ref = next(it), next(it), next(it), next(it), next(it)
    if dual:
        lq1, lk1, lq2, lk2, sub_ref = next(it), next(it), next(it), next(it), next(it)
    if has_sink:
        sink_ref = next(it)
    o_ref = next(it)
    qe_ref, m_ref, l_ref, acc_ref = next(it), next(it), next(it), next(it)

    step = pl.program_id(2)
    is_latent = step < n_lat_steps
    ncol = (2 if dual else heads) * tq
    if dual:
        q = q_ref[...]
        lane = lax.broadcasted_iota(jnp.int32, q.shape, 1)
        zero = jnp.zeros_like(q)
        qe_ref[0:tq, :] = jnp.where(lane < B_QK_DIM, q, zero)
        qe_ref[tq:2 * tq, :] = jnp.where(lane < B_QK_DIM, zero, q)
    else:
        for hh in range(heads):
            qe_ref[hh * tq:(hh + 1) * tq, :] = q_ref[:, hh * LANES:(hh + 1) * LANES]

    bound = bound_ref[0]
    use_bound = bound <= SAFE_LOG2_BOUND

    def scores(k, valid):
        s = lax.dot_general(k, qe_ref[...], (((1,), (1,)), ((), ())), preferred_element_type=F32)
        return s if valid is None else jnp.where(valid, s, NEG_INF)

    def dense_blocks():
        for c in range(lat_len // KC):
            yield klat_ref[c * KC:(c + 1) * KC, :], vlat_ref[0, :, c * KC:(c + 1) * KC], None

    def window_blocks():
        q0 = step * tq
        kk = lax.broadcasted_iota(jnp.int32, (WINDOW, ncol), 0)
        qq = lax.broadcasted_iota(jnp.int32, (WINDOW, ncol), 1) % tq
        rel = kk - qq
        nblk = lat_len // WINDOW
        for j in range(-1, tq // WINDOW + 1):
            blk = q0 // WINDOW + j
            start = pl.multiple_of(jnp.clip(blk, 0, nblk - 1) * WINDOW, WINDOW)
            dist = rel + j * WINDOW
            in_range = jnp.logical_and(blk >= 0, blk < nblk)
            valid = jnp.logical_and(jnp.abs(dist) <= WINDOW, in_range)
            yield klat_ref[pl.ds(start, WINDOW), :], vlat_ref[0, :, pl.ds(start, WINDOW)], valid

    def sink_logits():
        g = pl.program_id(1)
        return jnp.concatenate(
            [jnp.full((1, tq), sink_ref[g * heads + hh], F32) * LOG2E for hh in range(heads)], axis=1)

    def finalize(o):
        if dual:
            s1 = jnp.sum(lq1[...] * lk1[...], axis=-1, keepdims=True)
            s2 = jnp.sum(lq2[...] * lk2[...], axis=-1, keepdims=True)
            lam = jnp.exp(s1) - jnp.exp(s2) + lambda_init
            od = (o[:, 0:tq] - lam * o[:, tq:2 * tq]).T
            ms = jnp.mean(od * od, axis=-1, keepdims=True)
            od = od * lax.rsqrt(ms + EPS) * sub_ref[...] * (1.0 - lambda_init)
            o_ref[...] = od.astype(BF16)
        else:
            for hh in range(heads):
                o_ref[:, hh * LANES:(hh + 1) * LANES] = o[:, hh * tq:(hh + 1) * tq].T.astype(BF16)

    def bounded_pass():
        def contrib(k, vt, valid):
            p = jnp.exp2(scores(k, valid) - bound)
            return jnp.dot(vt, p.astype(BF16), preferred_element_type=F32), jnp.sum(p, axis=0, keepdims=True)

        def latent():
            acc = den = None
            for k, vt, valid in (window_blocks() if window else dense_blocks()):
                da, dl = contrib(k, vt, valid)
                acc = da if acc is None else acc + da
                den = dl if den is None else den + dl
            return acc, den

        def nothing():
            return jnp.zeros((LANES, ncol), F32), jnp.zeros((1, ncol), F32)

        acc_c, den_c = contrib(kctx_ref[...], vctx_ref[0], None)
        acc_l, den_l = lax.cond(is_latent, latent, nothing)
        den = den_c + den_l
        if has_sink:
            den = den + jnp.exp2(sink_logits() - bound)
        finalize((acc_c + acc_l) * (1.0 / den))

    def online_pass():
        if has_sink:
            m_ref[...] = sink_logits()
            l_ref[...] = jnp.ones((1, ncol), F32)
        else:
            m_ref[...] = jnp.full((1, ncol), NEG_INF, F32)
            l_ref[...] = jnp.zeros((1, ncol), F32)
        acc_ref[...] = jnp.zeros((LANES, ncol), F32)

        def chunk(k, vt, valid=None):
            s = scores(k, valid)
            m_old = m_ref[...]
            m_new = jnp.maximum(m_old, jnp.max(s, axis=0, keepdims=True))
            alpha = jnp.exp2(m_old - m_new)
            p = jnp.exp2(s - m_new)
            l_ref[...] = alpha * l_ref[...] + jnp.sum(p, axis=0, keepdims=True)
            acc_ref[...] = alpha * acc_ref[...] + jnp.dot(vt, p.astype(BF16), preferred_element_type=F32)
            m_ref[...] = m_new

        chunk(kctx_ref[...], vctx_ref[0])

        @pl.when(is_latent)
        def _():
            if window:
                for k, vt, valid in window_blocks():
                    chunk(k, vt, valid)
            else:
                def body(c, carry):
                    start = pl.multiple_of(c * KC, KC)
                    chunk(klat_ref[pl.ds(start, KC), :], vlat_ref[0, :, pl.ds(start, KC)])
                    return carry
                lax.fori_loop(0, lat_len // KC, body, 0)

        finalize(acc_ref[...] * (1.0 / l_ref[...]))

    pl.when(use_bound)(bounded_pass)
    pl.when(jnp.logical_not(use_bound))(online_pass)


def _score_bound(q_gain, k_gain, group, sink=None):
    bound = jnp.max(jnp.abs(q_gain)) * jnp.max(jnp.abs(k_gain)) * (math.sqrt(group) * LOG2E * BOUND_MARGIN)
    if sink is not None:
        bound = jnp.maximum(bound, jnp.max(sink) * LOG2E)
    return bound.reshape(1).astype(F32)


def _attention(bound, q, k_all, vt_all, *, n_batch, s_len, ctx_len, ctx_queries, heads, n_groups, dual, window,
               lambda_init=0.0, lam_params=None, subln=None, sink=None):
    assert ctx_len == TQ
    width = heads * LANES
    nq = s_len // TQ
    ctx_blk0 = n_batch * s_len // ctx_len
    ncol = (2 if dual else heads) * TQ
    qmap = lambda b, g, i: (jnp.where(i < nq, b * nq + i, ctx_blk0 + b), g)

    in_specs = [pl.BlockSpec(memory_space=pltpu.SMEM),
                pl.BlockSpec((TQ, width), qmap),
                pl.BlockSpec((s_len, LANES), lambda b, g, i: (b, g)),
                pl.BlockSpec((1, LANES, s_len), lambda b, g, i: (g, 0, b)),
                pl.BlockSpec((ctx_len, LANES), lambda b, g, i: (ctx_blk0 + b, g)),
                pl.BlockSpec((1, LANES, ctx_len), lambda b, g, i: (g, 0, ctx_blk0 + b))]
    args = [bound, q, k_all, vt_all, k_all, vt_all]
    if dual:
        in_specs += [pl.BlockSpec((1, B_QK_DIM), lambda b, g, i: (0, 0))] * 4
        in_specs += [pl.BlockSpec((1, LANES), lambda b, g, i: (0, 0))]
        args += [p.reshape(1, B_QK_DIM) for p in lam_params] + [subln.reshape(1, LANES)]
    if sink is not None:
        in_specs += [pl.BlockSpec(memory_space=pltpu.SMEM)]
        args += [sink]

    kern = functools.partial(_attn_kernel, heads=heads, dual=dual, tq=TQ, lat_len=s_len, n_lat_steps=nq,
                             window=window, has_sink=sink is not None, lambda_init=lambda_init)
    return pl.pallas_call(
        kern,
        grid=(n_batch, n_groups, nq + (1 if ctx_queries else 0)),
        in_specs=in_specs,
        out_specs=pl.BlockSpec((TQ, width), qmap),
        out_shape=jax.ShapeDtypeStruct((q.shape[0], n_groups * width), BF16),
        scratch_shapes=[pltpu.VMEM((ncol, LANES), BF16), pltpu.VMEM((1, ncol), F32),
                        pltpu.VMEM((1, ncol), F32), pltpu.VMEM((LANES, ncol), F32)],
        compiler_params=_params(("arbitrary", "arbitrary", "arbitrary")),
        name="attention",
    )(*args)


def _merge_kernel(oa_ref, ob_ref, oc_ref, g_ref, wa32, wb32, wc32, m_ref, wa_ref, wb_ref, wc_ref):
    first = pl.program_id(0) == 0
    _cast_weight_once(wa32, wa_ref, first)
    _cast_weight_once(wb32, wb_ref, first)
    _cast_weight_once(wc32, wc_ref, first)
    oa, ob, oc = oa_ref[...], ob_ref[...], oc_ref[...]
    step = 4 * LANES
    for c in range(0, D_MODEL, step):
        sl = slice(c, c + step)
        m = g_ref[:, c:c + step].astype(F32) * jnp.dot(oa, wa_ref[:, sl], preferred_element_type=F32)
        m = m + g_ref[:, D_MODEL + c:D_MODEL + c + step].astype(F32) * jnp.dot(
            ob, wb_ref[:, sl], preferred_element_type=F32)
        m = m + g_ref[:, 2 * D_MODEL + c:2 * D_MODEL + c + step].astype(F32) * jnp.dot(
            oc, wc_ref[:, sl], preferred_element_type=F32)
        m_ref[:, sl] = m.astype(BF16)


def _merge(layer, oa, ob, oc, gates, wa, wb, wc, nt):
    const = lambda i: (layer, 0, 0)
    return pl.pallas_call(
        _merge_kernel,
        grid=(nt,),
        in_specs=[pl.BlockSpec((TM, A_Q_W), lambda i: (i, 0)),
                  pl.BlockSpec((TM, B_V_W), lambda i: (i, 0)),
                  pl.BlockSpec((TM, C_Q_W), lambda i: (i, 0)),
                  pl.BlockSpec((TM, 3 * D_MODEL), lambda i: (i, 0)),
                  _resident_weight_spec((1, A_Q_W, D_MODEL), const),
                  _resident_weight_spec((1, B_V_W, D_MODEL), const),
                  _resident_weight_spec((1, C_Q_W, D_MODEL), const)],
        out_specs=pl.BlockSpec((TM, D_MODEL), lambda i: (i, 0)),
        out_shape=jax.ShapeDtypeStruct((nt * TM, D_MODEL), BF16),
        scratch_shapes=[pltpu.VMEM((A_Q_W, D_MODEL), BF16), pltpu.VMEM((B_V_W, D_MODEL), BF16),
                        pltpu.VMEM((C_Q_W, D_MODEL), BF16)],
        compiler_params=_params(("arbitrary",)),
        name="merge",
    )(oa, ob, oc, gates, wa, wb, wc)


def _out_kernel(m_ref, w32_ref, x_ref, mod_ref, o_ref, w_ref):
    _cast_weight_once(w32_ref, w_ref, pl.program_id(0) == 0)
    m = m_ref[...]
    step = 4 * LANES
    for c in range(0, D_MODEL, step):
        o = jnp.dot(m, w_ref[:, c:c + step], preferred_element_type=F32)
        o_ref[:, c:c + step] = x_ref[:, c:c + step] + mod_ref[0, 2:3, c:c + step] * o


def _out_proj(layer, m, w_out, x_all, mod3, nt, tiles_per_batch, n_batch):
    return pl.pallas_call(
        _out_kernel,
        grid=(nt,),
        in_specs=[pl.BlockSpec((TM, D_MODEL), lambda i: (i, 0)),
                  _resident_weight_spec((1, D_MODEL, D_MODEL), lambda i: (layer, 0, 0)),
                  pl.BlockSpec((TM, D_MODEL), lambda i: (i, 0)),
                  pl.BlockSpec((1, 8, D_MODEL), lambda i: (_group_of_tile(i, tiles_per_batch, n_batch), 0, 0))],
        out_specs=pl.BlockSpec((TM, D_MODEL), lambda i: (i, 0)),
        out_shape=jax.ShapeDtypeStruct((nt * TM, D_MODEL), F32),
        scratch_shapes=[pltpu.VMEM((D_MODEL, D_MODEL), BF16)],
        compiler_params=_params(("arbitrary",)),
        name="out_proj",
    )(m, w_out, x_all, mod3)


def _norm_router_kernel(x_ref, g_ref, mod_ref, wr_ref, h_ref, aff_ref):
    h = _modulated_norm(x_ref[...], g_ref[...], mod_ref[0], 3, 4)
    h_ref[...] = h
    logits = _dot3(wr_ref[...], h, ((1,), (1,)))
    e = jnp.exp(logits - jnp.max(logits, axis=0, keepdims=True))
    aff_ref[...] = e / jnp.sum(e, axis=0, keepdims=True)


def _norm_router(x_all, gain, mod3, w_router_t, nt, tiles_per_batch, n_batch):
    d = x_all.shape[1]
    return pl.pallas_call(
        _norm_router_kernel,
        grid=(nt,),
        in_specs=[pl.BlockSpec((TM, d), lambda i: (i, 0)),
                  pl.BlockSpec((1, d), lambda i: (0, 0)),
                  pl.BlockSpec((1, 8, d), lambda i: (_group_of_tile(i, tiles_per_batch, n_batch), 0, 0)),
                  pl.BlockSpec((N_EXPERTS, d), lambda i: (0, 0))],
        out_specs=[pl.BlockSpec((TM, d), lambda i: (i, 0)),
                   pl.BlockSpec((N_EXPERTS, TM), lambda i: (0, i))],
        out_shape=[jax.ShapeDtypeStruct((nt * TM, d), F32),
                   jax.ShapeDtypeStruct((N_EXPERTS, nt * TM), F32)],
        compiler_params=_params(("arbitrary",)),
        name="norm_router",
    )(x_all, gain.reshape(1, d), mod3, w_router_t)


def _prefix_count(mask, n):
    ii = lax.broadcasted_iota(jnp.int32, (LANES, LANES), 0)
    jj = lax.broadcasted_iota(jnp.int32, (LANES, LANES), 1)
    tri = jnp.where(ii <= jj, 1.0, 0.0).astype(BF16)
    m = jnp.where(mask, 1.0, 0.0).astype(BF16)
    outs, starts = [], []
    off = jnp.zeros((mask.shape[0], 1), F32)
    for b in range(n // LANES):
        if (b * LANES) % TT == 0:
            starts.append(off)
        c = jnp.dot(m[:, b * LANES:(b + 1) * LANES], tri, preferred_element_type=F32) + off
        outs.append(c)
        off = c[:, LANES - 1:LANES]
    return jnp.concatenate(outs, axis=1), jnp.concatenate(starts, axis=1)


def _route_kernel(aff_ref, idx_ref, gate_ref, start_ref, csel_ref, *, n, cap, row_base):
    a = aff_ref[...]
    bits = pltpu.bitcast(a, jnp.int32)
    kf = float(cap)

    def search(i, t):
        cand = t | jnp.left_shift(jnp.int32(1), 30 - i)
        cnt = jnp.sum(jnp.where(bits >= cand, 1.0, 0.0), axis=1, keepdims=True)
        return jnp.where(cnt >= kf, cand, t)

    thr = lax.fori_loop(0, 31, search, jnp.zeros((N_EXPERTS, 1), jnp.int32))
    above = bits > thr
    equal = bits == thr
    need = kf - jnp.sum(jnp.where(above, 1.0, 0.0), axis=1, keepdims=True)
    chosen = jnp.logical_or(above, jnp.logical_and(equal, _prefix_count(equal, n)[0] <= need))
    slot_no, tile_starts = _prefix_count(chosen, n)
    csel_ref[...] = jnp.where(chosen, slot_no, 0.0)
    start_ref[0] = tile_starts.astype(jnp.int32)

    tok = lax.broadcasted_iota(jnp.int32, (1, n), 1).astype(F32)
    sub = min(cap, 64)
    base = row_base + pl.program_id(0) * n

    def per_expert(e, carry):
        ce = csel_ref[pl.ds(e, 1), :]
        ae = aff_ref[pl.ds(e, 1), :]
        for s0 in range(0, cap, sub):
            slot = (lax.broadcasted_iota(jnp.int32, (sub, n), 0) + (s0 + 1)).astype(F32)
            hit = ce == slot
            tsel = jnp.sum(jnp.where(hit, tok, 0.0), axis=1, keepdims=True)
            gsel = jnp.sum(jnp.where(hit, ae, 0.0), axis=1, keepdims=True)
            idx_ref[0, e, s0:s0 + sub, :] = tsel.astype(jnp.int32) + base
            gate_ref[0, e, s0:s0 + sub, :] = gsel
        return carry

    lax.fori_loop(0, N_EXPERTS, per_expert, 0)


def _route(aff_t, n_sets, n, cap, first_block, row_base):
    return pl.pallas_call(
        functools.partial(_route_kernel, n=n, cap=cap, row_base=row_base),
        grid=(n_sets,),
        in_specs=[pl.BlockSpec((N_EXPERTS, n), lambda s: (0, first_block + s))],
        out_specs=[pl.BlockSpec((1, N_EXPERTS, cap, 1), lambda s: (s, 0, 0, 0)),
                   pl.BlockSpec((1, N_EXPERTS, cap, 1), lambda s: (s, 0, 0, 0)),
                   pl.BlockSpec((1, N_EXPERTS, n // TT), lambda s: (s, 0, 0))],
        out_shape=[jax.ShapeDtypeStruct((n_sets, N_EXPERTS, cap, 1), jnp.int32),
                   jax.ShapeDtypeStruct((n_sets, N_EXPERTS, cap, 1), F32),
                   jax.ShapeDtypeStruct((n_sets, N_EXPERTS, n // TT), jnp.int32)],
        scratch_shapes=[pltpu.VMEM((N_EXPERTS, n), F32)],
        compiler_params=_params(("arbitrary",)),
        name="route",
    )(aff_t)


def _ffn_kernel(rows_ref, h_hbm, wg_ref, wu_ref, wd_ref, gate_ref, rowid_ref, mod_ref, y_ref,
                buf, xg, sem, *, nr, n_ff, groups):
    e = pl.program_id(0)
    f = pl.program_id(1)
    base = e * nr
    d = h_hbm.shape[1]

    @pl.when(f == 0)
    def _():
        def issue(i, carry):
            r = rows_ref[base + i]
            pltpu.make_async_copy(h_hbm.at[pl.ds(r, 1), :], buf.at[pl.ds(i, 1), :], sem).start()
            return carry

        lax.fori_loop(0, nr, issue, 0, unroll=8)
        pltpu.make_async_copy(h_hbm.at[pl.ds(0, nr), :], buf, sem).wait()
        xg[...] = buf[...].astype(BF16)

    x = xg[...]
    u = jnp.dot(x, wg_ref[0, 0].astype(BF16), preferred_element_type=F32)
    v = jnp.dot(x, wu_ref[0, 0].astype(BF16), preferred_element_type=F32)
    act = (u * jax.nn.sigmoid(u) * v).astype(BF16)
    contrib = jnp.dot(act, wd_ref[0, 0].astype(BF16), preferred_element_type=F32)

    @pl.when(f == 0)
    def _():
        buf[...] = contrib

    @pl.when(f > 0)
    def _():
        buf[...] += contrib

    @pl.when(f == n_ff - 1)
    def _():
        for (r0, r1, grp) in groups:
            y_ref[0, r0:r1, 0:d] = (buf[r0:r1, :] * gate_ref[0, r0:r1, :] * mod_ref[grp, 5:6, :]).astype(BF16)
        rid = rowid_ref[0]
        lane = lax.broadcasted_iota(jnp.int32, (nr, LANES), 1)
        tag = jnp.where(lane == 0, rid // TT, jnp.where(lane == 1, rid % TT, 0))
        y_ref[0, 0:nr, d:d + LANES] = tag.astype(F32).astype(BF16)
        y_ref[0, nr:nr + WIN, :] = jnp.zeros((WIN, d + LANES), BF16)


def _expert_ffn(layer, rows, h2, w_gate, w_up, w_down, gates, mod3, nr, groups):
    d = h2.shape[1]
    n_ff = EXPERT_FF // FF_TILE
    kern = functools.partial(_ffn_kernel, nr=nr, n_ff=n_ff, groups=groups)
    return pl.pallas_call(
        kern,
        grid_spec=pltpu.PrefetchScalarGridSpec(
            num_scalar_prefetch=1,
            grid=(N_EXPERTS, n_ff),
            in_specs=[pl.BlockSpec(memory_space=pl.ANY),
                      pl.BlockSpec((1, 1, d, FF_TILE), lambda e, f, rows: (layer, e, 0, f)),
                      pl.BlockSpec((1, 1, d, FF_TILE), lambda e, f, rows: (layer, e, 0, f)),
                      pl.BlockSpec((1, 1, FF_TILE, d), lambda e, f, rows: (layer, e, f, 0)),
                      pl.BlockSpec((1, nr, 1), lambda e, f, rows: (e, 0, 0)),
                      pl.BlockSpec((1, nr, 1), lambda e, f, rows: (e, 0, 0)),
                      pl.BlockSpec((3, 8, d), lambda e, f, rows: (0, 0, 0))],
            out_specs=pl.BlockSpec((1, nr + WIN, d + LANES), lambda e, f, rows: (e, 0, 0)),
            scratch_shapes=[pltpu.VMEM((nr, d), F32), pltpu.VMEM((nr, d), BF16), pltpu.SemaphoreType.DMA(())]),
        out_shape=jax.ShapeDtypeStruct((N_EXPERTS, nr + WIN, d + LANES), BF16),
        compiler_params=_params(("arbitrary", "arbitrary")),
        name="expert_ffn",
    )(rows.reshape(-1), h2, w_gate, w_up, w_down, gates, rows.reshape(N_EXPERTS, nr, 1), mod3)


def _combine_kernel(starts_ref, rounds_ref, y_hbm, x_ref, o_ref, stage, extra, sem, sem_x, *, n_tiles, nr):
    j = pl.program_id(0)
    slot = j % 2
    d = x_ref.shape[1]

    def windows(tile, rnd, dst, dsem):
        copies = []
        for e in range(N_EXPERTS):
            first = starts_ref[tile * N_EXPERTS + e]
            row = jnp.minimum(first - first % ROW_ALIGN + rnd * WIN, nr)
            copies.append(pltpu.make_async_copy(y_hbm.at[e, pl.ds(pl.multiple_of(row, ROW_ALIGN), WIN), :],
                                                dst.at[pl.ds(e * WIN, WIN), :], dsem))
        return copies

    @pl.when(j == 0)
    def _():
        for cp in windows(0, 0, stage.at[0], sem.at[0]):
            cp.start()

    @pl.when(j + 1 < n_tiles)
    def _():
        for cp in windows(j + 1, 0, stage.at[1 - slot], sem.at[1 - slot]):
            cp.start()

    for cp in windows(j, 0, stage.at[slot], sem.at[slot]):
        cp.wait()

    tok = (lax.broadcasted_iota(jnp.int32, (1, TT), 1) + j * TT).astype(F32)

    def placed(rows):
        ids = rows[:, d:d + 1].astype(F32) * float(TT) + rows[:, d + 1:d + 2].astype(F32)
        onehot = jnp.where(ids == tok, 1.0, 0.0).astype(BF16)
        return lax.dot_general(onehot, rows[:, 0:d], (((0,), (0,)), ((), ())), preferred_element_type=F32)

    o_ref[...] = x_ref[...] + placed(stage[slot])

    def more(rnd, carry):
        cps = windows(j, rnd, extra, sem_x)
        for cp in cps:
            cp.start()
        for cp in cps:
            cp.wait()
        o_ref[...] += placed(extra[...])
        return carry

    lax.fori_loop(1, rounds_ref[j], more, 0)


def _combine(starts, rounds, y, x_mid, n_tiles, nr):
    d = x_mid.shape[1]
    return pl.pallas_call(
        functools.partial(_combine_kernel, n_tiles=n_tiles, nr=nr),
        grid_spec=pltpu.PrefetchScalarGridSpec(
            num_scalar_prefetch=2,
            grid=(n_tiles,),
            in_specs=[pl.BlockSpec(memory_space=pl.ANY),
                      pl.BlockSpec((TT, d), lambda j, s, r: (j, 0))],
            out_specs=pl.BlockSpec((TT, d), lambda j, s, r: (j, 0)),
            scratch_shapes=[pltpu.VMEM((2, N_EXPERTS * WIN, d + LANES), BF16),
                            pltpu.VMEM((N_EXPERTS * WIN, d + LANES), BF16),
                            pltpu.SemaphoreType.DMA((2,)), pltpu.SemaphoreType.DMA(())]),
        out_shape=jax.ShapeDtypeStruct((n_tiles * TT, d), F32),
        compiler_params=_params(("arbitrary",)),
        name="combine",
    )(starts, rounds, y, x_mid)


def kernel(x, c, ctx, c_ctx, w_mod, b_mod, norm_mix, norm_ffn, w_in, qn_a, kn_a, qn_b, kn_b,
           lam_q1, lam_k1, lam_q2, lam_k2, subln_b, qn_c, kn_c, sink_c,
           w_br_a, w_br_b, w_br_c, w_out, w_router, w_gate, w_up, w_down):
    n_batch, s_len, d = x.shape
    ctx_len = ctx.shape[1]
    depth = w_mod.shape[0]
    lat_rows = n_batch * s_len
    tiles_per_batch = s_len // TM
    n_lat_tiles = lat_rows // TM
    assert s_len % TM == 0 and (n_batch * ctx_len) == TM and s_len % KC == 0 and d == D_MODEL
    assert KV_W == Q_W == D_MODEL and ctx_len % TT == 0 and TM % TT == 0

    cvec = jnp.zeros((8, d), F32).at[:n_batch].set(c).at[n_batch].set(c_ctx)
    mod_all = _modulation(cvec, w_mod, b_mod)

    tabs = _rope_tables(s_len, HEAD_DIM, TM) + _rope_tables(s_len, B_QK_DIM, TM)
    x_all = jnp.concatenate([x.reshape(lat_rows, d), ctx.reshape(n_batch * ctx_len, d)], axis=0)

    cap_lat = EC_CAPACITY * s_len // N_EXPERTS
    cap_ctx = EC_CAPACITY * ctx_len // N_EXPERTS
    lane_vec = lambda v: v.reshape(1, -1) if v.shape[-1] == LANES else jnp.tile(v, LANES // v.shape[-1]).reshape(1, LANES)

    for l in range(depth):
        last = l == depth - 1
        lambda_init = 0.8 - 0.6 * math.exp(-0.3 * l)
        nt = n_lat_tiles if last else n_lat_tiles + 1
        mod3 = jnp.pad(mod_all[l, :n_batch + 1].reshape(n_batch + 1, 6, d), ((0, 0), (0, 2), (0, 0)))

        h = _norm_mix(x_all, norm_mix[l], mod3, tiles_per_batch, n_batch)
        ka, kb, kc, vta, vtb, vtc = _kv_proj(l, h, w_in, tabs, lane_vec(kn_a[l]), lane_vec(kn_b[l]),
                                             lane_vec(kn_c[l]), tiles_per_batch, n_lat_tiles)
        qa, qb, qc = _q_proj(l, h, w_in, tabs, lane_vec(qn_a[l]), lane_vec(qn_b[l]),
                             lane_vec(qn_c[l]), tiles_per_batch, n_lat_tiles, nt)
        gates = _gate_proj(l, h, w_in, nt)

        common = dict(n_batch=n_batch, s_len=s_len, ctx_len=ctx_len, ctx_queries=not last)
        lam_params = (lam_q1[l], lam_k1[l], lam_q2[l], lam_k2[l])
        oa = _attention(_score_bound(qn_a[l], kn_a[l], HEAD_DIM), qa, ka, vta, heads=A_Q_HEADS // A_KV_HEADS,
                        n_groups=A_KV_HEADS, dual=False, window=False, **common)
        ob = _attention(_score_bound(qn_b[l], kn_b[l], B_QK_DIM), qb, kb, vtb, heads=1, n_groups=B_HEADS, dual=True,
                        window=False, lambda_init=lambda_init, lam_params=lam_params, subln=subln_b[l], **common)
        oc = _attention(_score_bound(qn_c[l], kn_c[l], HEAD_DIM, sink_c[l]), qc, kc, vtc,
                        heads=C_Q_HEADS // C_KV_HEADS, n_groups=C_KV_HEADS, dual=False, window=True,
                        sink=sink_c[l], **common)

        m = _merge(l, oa, ob, oc, gates, w_br_a, w_br_b, w_br_c, nt)
        x_mid = _out_proj(l, m, w_out, x_all, mod3, nt, tiles_per_batch, n_batch)

        h2, aff_t = _norm_router(x_mid, norm_ffn[l], mod3, w_router[l].T, nt, tiles_per_batch, n_batch)
        idx_lat, gate_lat, st_lat = _route(aff_t, n_batch, s_len, cap_lat, 0, 0)
        sets = [(idx_lat[b], gate_lat[b], st_lat[b], cap_lat, b) for b in range(n_batch)]
        if not last:
            idx_ctx, gate_ctx, st_ctx = _route(aff_t, n_batch, ctx_len, cap_ctx, lat_rows // ctx_len, lat_rows)
            sets += [(idx_ctx[b], gate_ctx[b], st_ctx[b], cap_ctx, n_batch) for b in range(n_batch)]
        idx_parts, gate_parts, groups, tile_lo, tile_hi, off = [], [], [], [], [], 0
        for idx, gate, st, cap, grp in sets:
            idx_parts.append(idx[:, :, 0])
            gate_parts.append(gate)
            groups.append((off, off + cap, grp))
            tile_lo.append(st.T + off)
            tile_hi.append(jnp.concatenate([st[:, 1:], jnp.full((N_EXPERTS, 1), cap, jnp.int32)], axis=1).T + off)
            off += cap
        rows = jnp.concatenate(idx_parts, axis=1)
        gate_rows = jnp.concatenate(gate_parts, axis=1)
        starts = jnp.concatenate(tile_lo, axis=0)
        counts = jnp.concatenate(tile_hi, axis=0) - starts
        rounds = jnp.max((starts % ROW_ALIGN + counts + (WIN - 1)) // WIN, axis=1)
        y = _expert_ffn(l, rows, h2, w_gate, w_up, w_down, gate_rows, mod3, off, tuple(groups))
        x_all = _combine(starts.reshape(-1), rounds, y, x_mid, nt * TM // TT, off)

    return x_all[:lat_rows].reshape(n_batch, s_len, d)
```

```python
import functools
import math

import jax
import jax.numpy as jnp
from jax import lax
from jax.experimental import pallas as pl
from jax.experimental.pallas import tpu as pltpu

F32 = jnp.float32
BF16 = jnp.bfloat16

D_MODEL = 2048
GRID_W = 64
HEAD_DIM = 128
A_Q_HEADS, A_KV_HEADS = 6, 2
B_HEADS, B_QK_DIM = 4, 64
C_Q_HEADS, C_KV_HEADS = 6, 2
WINDOW = 128
N_EXPERTS = 16
EXPERT_FF = 1024
EC_CAPACITY = 2
ROPE_THETA = 10000.0
EPS = 1e-6
NEG_INF = -1e30

A_Q_W = A_Q_HEADS * HEAD_DIM
A_KV_W = A_KV_HEADS * HEAD_DIM
B_QK_W = B_HEADS * 2 * B_QK_DIM
B_V_W = B_HEADS * HEAD_DIM
C_Q_W = C_Q_HEADS * HEAD_DIM
C_KV_W = C_KV_HEADS * HEAD_DIM
KV_W = 2 * A_KV_W + B_QK_W + B_V_W + 2 * C_KV_W
Q_W = A_Q_W + B_QK_W + C_Q_W

LANES = 128
TM = 512
TQ = 256
KC = 512
FF_TILE = 256
N_UP = 4
OUT_TILE = 512
N_DOWN = 4
TT = 256
WIN = 64
ROW_ALIGN = 16
VMEM_LIMIT = 56 * 1024 * 1024
LOG2E = 1.4426950408889634
SAFE_LOG2_BOUND = 60.0
BOUND_MARGIN = 1.03


def _params(sem, vmem=VMEM_LIMIT):
    return pltpu.CompilerParams(dimension_semantics=sem, vmem_limit_bytes=vmem)


def _split_bf16(x):
    hi = x.astype(BF16)
    lo = (x - hi.astype(F32)).astype(BF16)
    return hi, lo


def _dot3(a, b, dims):
    ah, al = _split_bf16(a)
    bh, bl = _split_bf16(b)
    dg = lambda x, y: lax.dot_general(x, y, (dims, ((), ())), preferred_element_type=F32)
    return dg(ah, bh) + (dg(ah, bl) + dg(al, bh))


def _resident_weight_spec(block, index_map):
    return pl.BlockSpec(block, index_map, pipeline_mode=pl.Buffered(1))


def _cast_weight_once(w_ref, wbf_ref, first):
    @pl.when(first)
    def _():
        wbf_ref[...] = w_ref[0].astype(BF16)


def _mod_kernel(c_ref, w_ref, b_ref, o_ref):
    c = c_ref[...]
    a = c * jax.nn.sigmoid(c)
    o_ref[0] = _dot3(a, w_ref[0], ((1,), (0,))) + b_ref[0]


def _modulation(cvec, w_mod, b_mod):
    depth, d, n = w_mod.shape
    tn = 1024
    return pl.pallas_call(
        _mod_kernel,
        grid=(depth, n // tn),
        in_specs=[pl.BlockSpec((8, d), lambda l, j: (0, 0)),
                  pl.BlockSpec((1, d, tn), lambda l, j: (l, 0, j)),
                  pl.BlockSpec((1, 1, tn), lambda l, j: (l, 0, j))],
        out_specs=pl.BlockSpec((1, 8, tn), lambda l, j: (l, 0, j)),
        out_shape=jax.ShapeDtypeStruct((depth, 8, n), F32),
        compiler_params=_params(("arbitrary", "arbitrary")),
        name="modulation",
    )(cvec, w_mod, b_mod.reshape(depth, 1, n))


def _modulated_norm(x, gain, mod, shift_row, scale_row):
    ms = jnp.mean(x * x, axis=-1, keepdims=True)
    y = x * lax.rsqrt(ms + EPS) * gain
    return y * (1.0 + mod[scale_row:scale_row + 1, :]) + mod[shift_row:shift_row + 1, :]


def _norm_kernel(x_ref, g_ref, mod_ref, h_ref):
    h_ref[...] = _modulated_norm(x_ref[...], g_ref[...], mod_ref[0], 0, 1).astype(BF16)


def _group_of_tile(i, tiles_per_batch, n_batch):
    return jnp.minimum(i // tiles_per_batch, n_batch)


def _norm_mix(x_all, gain, mod3, tiles_per_batch, n_batch):
    r, d = x_all.shape
    return pl.pallas_call(
        _norm_kernel,
        grid=(r // TM,),
        in_specs=[pl.BlockSpec((TM, d), lambda i: (i, 0)),
                  pl.BlockSpec((1, d), lambda i: (0, 0)),
                  pl.BlockSpec((1, 8, d), lambda i: (_group_of_tile(i, tiles_per_batch, n_batch), 0, 0))],
        out_specs=pl.BlockSpec((TM, d), lambda i: (i, 0)),
        out_shape=jax.ShapeDtypeStruct((r, d), BF16),
        compiler_params=_params(("arbitrary",)),
        name="norm_mix",
    )(x_all, gain.reshape(1, d), mod3)


def _rope_tables(s_len, dh, pad_rows):
    d_ax = dh // 2
    inv = ROPE_THETA ** (-jnp.arange(0, d_ax, 2, dtype=F32) / d_ax)
    t = jnp.arange(s_len, dtype=jnp.int32)
    fr = (t // GRID_W).astype(F32)[:, None] * inv
    fc = (t % GRID_W).astype(F32)[:, None] * inv
    ang = jnp.concatenate([fr, fr, fc, fc], axis=-1)
    cos, sin = jnp.cos(ang), jnp.sin(ang)
    quarter = jnp.arange(dh) // (dh // 4)
    sin_a = jnp.where(quarter % 2 == 0, -sin, 0.0)
    sin_b = jnp.where(quarter % 2 == 1, sin, 0.0)
    reps = LANES // dh

    def finish(tab, ident):
        tab = jnp.tile(tab, (1, reps))
        return jnp.concatenate([tab, jnp.full((pad_rows, LANES), ident, F32)], axis=0)

    return finish(cos, 1.0), finish(sin_a, 0.0), finish(sin_b, 0.0)


def _norm_rope(t, gain, cos, sin_a, sin_b, group, scale):
    sq = t * t
    if group == LANES:
        ms = jnp.mean(sq, axis=-1, keepdims=True)
    else:
        lane = lax.broadcasted_iota(jnp.int32, t.shape, 1)
        low = lane < group
        s_lo = jnp.sum(jnp.where(low, sq, 0.0), axis=-1, keepdims=True)
        s_hi = jnp.sum(jnp.where(low, 0.0, sq), axis=-1, keepdims=True)
        ms = jnp.where(low, s_lo, s_hi) * (1.0 / group)
    y = t * lax.rsqrt(ms + EPS) * gain
    q = group // 4
    r = y * cos + pltpu.roll(y, LANES - q, 1) * sin_a + pltpu.roll(y, q, 1) * sin_b
    if scale != 1.0:
        r = r * scale
    return r


def _kv_kernel(h_ref, w32_ref, c128, a128, b128, c64, a64, b64, gka, gkb, gkc,
               ka_ref, kb_ref, kc_ref, vta_ref, vtb_ref, vtc_ref, w_ref):
    _cast_weight_once(w32_ref, w_ref, pl.program_id(0) == 0)
    h = h_ref[...]

    def mm(c0):
        return jnp.dot(h, w_ref[:, c0:c0 + 2 * LANES], preferred_element_type=F32)

    def keys(c0, out_ref, o0, gain, cos, sa, sb, group):
        acc = mm(c0)
        for j in range(2):
            blk = acc[:, j * LANES:(j + 1) * LANES]
            out_ref[:, o0 + j * LANES:o0 + (j + 1) * LANES] = _norm_rope(
                blk, gain[...], cos[...], sa[...], sb[...], group, 1.0).astype(BF16)

    def values(c0, out_ref, h0):
        acc = mm(c0)
        for j in range(2):
            out_ref[h0 + j] = acc[:, j * LANES:(j + 1) * LANES].T.astype(BF16)

    c = 0
    keys(c, ka_ref, 0, gka, c128, a128, b128, HEAD_DIM)
    c += A_KV_W
    values(c, vta_ref, 0)
    c += A_KV_W
    for j in range(B_QK_W // (2 * LANES)):
        keys(c, kb_ref, j * 2 * LANES, gkb, c64, a64, b64, B_QK_DIM)
        c += 2 * LANES
    for j in range(B_V_W // (2 * LANES)):
        values(c, vtb_ref, 2 * j)
        c += 2 * LANES
    keys(c, kc_ref, 0, gkc, c128, a128, b128, HEAD_DIM)
    c += C_KV_W
    values(c, vtc_ref, 0)


def _table_specs(pos_tiles, n_lat_tiles):
    spec = pl.BlockSpec((TM, LANES), lambda i: (jnp.where(i < n_lat_tiles, i % pos_tiles, pos_tiles), 0))
    return [spec] * 6


def _kv_proj(layer, h, w_in, tabs, gka, gkb, gkc, pos_tiles, n_lat_tiles):
    r, d = h.shape
    nt = r // TM
    vec = pl.BlockSpec((1, LANES), lambda i: (0, 0))
    return pl.pallas_call(
        _kv_kernel,
        grid=(nt,),
        in_specs=[pl.BlockSpec((TM, d), lambda i: (i, 0)),
                  _resident_weight_spec((1, d, KV_W), lambda i: (layer, 0, 0))]
        + _table_specs(pos_tiles, n_lat_tiles) + [vec] * 3,
        out_specs=[pl.BlockSpec((TM, A_KV_W), lambda i: (i, 0)),
                   pl.BlockSpec((TM, B_QK_W), lambda i: (i, 0)),
                   pl.BlockSpec((TM, C_KV_W), lambda i: (i, 0)),
                   pl.BlockSpec((A_KV_HEADS, LANES, TM), lambda i: (0, 0, i)),
                   pl.BlockSpec((B_HEADS, LANES, TM), lambda i: (0, 0, i)),
                   pl.BlockSpec((C_KV_HEADS, LANES, TM), lambda i: (0, 0, i))],
        out_shape=[jax.ShapeDtypeStruct((r, A_KV_W), BF16),
                   jax.ShapeDtypeStruct((r, B_QK_W), BF16),
                   jax.ShapeDtypeStruct((r, C_KV_W), BF16),
                   jax.ShapeDtypeStruct((A_KV_HEADS, LANES, r), BF16),
                   jax.ShapeDtypeStruct((B_HEADS, LANES, r), BF16),
                   jax.ShapeDtypeStruct((C_KV_HEADS, LANES, r), BF16)],
        scratch_shapes=[pltpu.VMEM((d, KV_W), BF16)],
        compiler_params=_params(("arbitrary",)),
        name="kv_proj",
    )(h, w_in, *tabs, gka, gkb, gkc)


def _q_kernel(h_ref, w32_ref, c128, a128, b128, c64, a64, b64, gqa, gqb, gqc, qa_ref, qb_ref, qc_ref, w_ref):
    _cast_weight_once(w32_ref, w_ref, pl.program_id(0) == 0)
    h = h_ref[...]

    def queries(c0, out_ref, o0, gain, cos, sa, sb, group):
        acc = jnp.dot(h, w_ref[:, c0:c0 + 2 * LANES], preferred_element_type=F32)
        for j in range(2):
            blk = acc[:, j * LANES:(j + 1) * LANES]
            out_ref[:, o0 + j * LANES:o0 + (j + 1) * LANES] = _norm_rope(
                blk, gain[...], cos[...], sa[...], sb[...], group, group ** -0.5 * LOG2E).astype(BF16)

    c = 0
    for j in range(A_Q_W // (2 * LANES)):
        queries(c, qa_ref, j * 2 * LANES, gqa, c128, a128, b128, HEAD_DIM)
        c += 2 * LANES
    for j in range(B_QK_W // (2 * LANES)):
        queries(c, qb_ref, j * 2 * LANES, gqb, c64, a64, b64, B_QK_DIM)
        c += 2 * LANES
    for j in range(C_Q_W // (2 * LANES)):
        queries(c, qc_ref, j * 2 * LANES, gqc, c128, a128, b128, HEAD_DIM)
        c += 2 * LANES


def _q_proj(layer, h, w_in, tabs, gqa, gqb, gqc, pos_tiles, n_lat_tiles, nt):
    d = h.shape[1]
    rows = nt * TM
    vec = pl.BlockSpec((1, LANES), lambda i: (0, 0))
    return pl.pallas_call(
        _q_kernel,
        grid=(nt,),
        in_specs=[pl.BlockSpec((TM, d), lambda i: (i, 0)),
                  _resident_weight_spec((1, d, Q_W), lambda i: (layer, 0, KV_W // Q_W))]
        + _table_specs(pos_tiles, n_lat_tiles) + [vec] * 3,
        out_specs=[pl.BlockSpec((TM, A_Q_W), lambda i: (i, 0)),
                   pl.BlockSpec((TM, B_QK_W), lambda i: (i, 0)),
                   pl.BlockSpec((TM, C_Q_W), lambda i: (i, 0))],
        out_shape=[jax.ShapeDtypeStruct((rows, A_Q_W), BF16),
                   jax.ShapeDtypeStruct((rows, B_QK_W), BF16),
                   jax.ShapeDtypeStruct((rows, C_Q_W), BF16)],
        scratch_shapes=[pltpu.VMEM((d, Q_W), BF16)],
        compiler_params=_params(("arbitrary",)),
        name="q_proj",
    )(h, w_in, *tabs, gqa, gqb, gqc)


def _gate_kernel(h_ref, w32_ref, g_ref, w_ref):
    _cast_weight_once(w32_ref, w_ref, pl.program_id(1) == 0)
    h = h_ref[...]
    for c in range(0, D_MODEL, 2 * LANES):
        z = jnp.dot(h, w_ref[:, c:c + 2 * LANES], preferred_element_type=F32)
        g_ref[:, c:c + 2 * LANES] = jax.nn.sigmoid(z).astype(BF16)


def _gate_proj(layer, h, w_in, nt):
    d = h.shape[1]
    return pl.pallas_call(
        _gate_kernel,
        grid=(3, nt),
        in_specs=[pl.BlockSpec((TM, d), lambda j, i: (i, 0)),
                  _resident_weight_spec((1, d, D_MODEL), lambda j, i: (layer, 0, (KV_W + Q_W) // D_MODEL + j))],
        out_specs=pl.BlockSpec((TM, D_MODEL), lambda j, i: (i, j)),
        out_shape=jax.ShapeDtypeStruct((nt * TM, 3 * D_MODEL), BF16),
        scratch_shapes=[pltpu.VMEM((d, D_MODEL), BF16)],
        compiler_params=_params(("arbitrary", "arbitrary")),
        name="gate_proj",
    )(h, w_in)


def _attn_kernel(*refs, heads, dual, tq, lat_len, n_lat_steps, window, has_sink, lambda_init):
    it = iter(refs)
    bound_ref = next(it)
    q_ref, klat_ref, vlat_ref, kctx_ref, vctx_ref = next(it), next(it), next(it), next(it), next(it)
    if dual:
        lq1, lk1, lq2, lk2, sub_ref = next(it), next(it), next(it), next(it), next(it)
    if has_sink:
        sink_ref = next(it)
    o_ref = next(it)
    qe_ref, m_ref, l_ref, acc_ref = next(it), next(it), next(it), next(it)

    step = pl.program_id(2)
    is_latent = step < n_lat_steps
    ncol = (2 if dual else heads) * tq
    if dual:
        q = q_ref[...]
        lane = lax.broadcasted_iota(jnp.int32, q.shape, 1)
        zero = jnp.zeros_like(q)
        qe_ref[0:tq, :] = jnp.where(lane < B_QK_DIM, q, zero)
        qe_ref[tq:2 * tq, :] = jnp.where(lane < B_QK_DIM, zero, q)
    else:
        for hh in range(heads):
            qe_ref[hh * tq:(hh + 1) * tq, :] = q_ref[:, hh * LANES:(hh + 1) * LANES]

    bound = bound_ref[0]
    use_bound = bound <= SAFE_LOG2_BOUND

    def scores(k, valid):
        s = lax.dot_general(k, qe_ref[...], (((1,), (1,)), ((), ())), preferred_element_type=F32)
        return s if valid is None else jnp.where(valid, s, NEG_INF)

    def dense_blocks():
        for c in range(lat_len // KC):
            yield klat_ref[c * KC:(c + 1) * KC, :], vlat_ref[0, :, c * KC:(c + 1) * KC], None

    def window_blocks():
        q0 = step * tq
        kk = lax.broadcasted_iota(jnp.int32, (WINDOW, ncol), 0)
        qq = lax.broadcasted_iota(jnp.int32, (WINDOW, ncol), 1) % tq
        rel = kk - qq
        nblk = lat_len // WINDOW
        for j in range(-1, tq // WINDOW + 1):
            blk = q0 // WINDOW + j
            start = pl.multiple_of(jnp.clip(blk, 0, nblk - 1) * WINDOW, WINDOW)
            dist = rel + j * WINDOW
            in_range = jnp.logical_and(blk >= 0, blk < nblk)
            valid = jnp.logical_and(jnp.abs(dist) <= WINDOW, in_range)
            yield klat_ref[pl.ds(start, WINDOW), :], vlat_ref[0, :, pl.ds(start, WINDOW)], valid

    def sink_logits():
        g = pl.program_id(1)
        return jnp.concatenate(
            [jnp.full((1, tq), sink_ref[g * heads + hh], F32) * LOG2E for hh in range(heads)], axis=1)

    def finalize(o):
        if dual:
            s1 = jnp.sum(lq1[...] * lk1[...], axis=-1, keepdims=True)
            s2 = jnp.sum(lq2[...] * lk2[...], axis=-1, keepdims=True)
            lam = jnp.exp(s1) - jnp.exp(s2) + lambda_init
            od = (o[:, 0:tq] - lam * o[:, tq:2 * tq]).T
            ms = jnp.mean(od * od, axis=-1, keepdims=True)
            od = od * lax.rsqrt(ms + EPS) * sub_ref[...] * (1.0 - lambda_init)
            o_ref[...] = od.astype(BF16)
        else:
            for hh in range(heads):
                o_ref[:, hh * LANES:(hh + 1) * LANES] = o[:, hh * tq:(hh + 1) * tq].T.astype(BF16)

    def bounded_pass():
        def weigh(s, vt):
            p = jnp.exp2(s - bound)
            return jnp.dot(vt, p.astype(BF16), preferred_element_type=F32), jnp.sum(p, axis=0, keepdims=True)

        def context_only():
            return weigh(scores(kctx_ref[...], None), vctx_ref[0])

        def with_latent():
            acc = den = None
            pending = (scores(kctx_ref[...], None), vctx_ref[0])
            for k, vt, valid in (window_blocks() if window else dense_blocks()):
                s_next = scores(k, valid)
                da, dl = weigh(*pending)
                acc = da if acc is None else acc + da
                den = dl if den is None else den + dl
                pending = (s_next, vt)
            da, dl = weigh(*pending)
            return acc + da, den + dl

        acc, den = lax.cond(is_latent, with_latent, context_only)
        if has_sink:
            den = den + jnp.exp2(sink_logits() - bound)
        finalize(acc * (1.0 / den))

    def online_pass():
        if has_sink:
            m_ref[...] = sink_logits()
            l_ref[...] = jnp.ones((1, ncol), F32)
        else:
            m_ref[...] = jnp.full((1, ncol), NEG_INF, F32)
            l_ref[...] = jnp.zeros((1, ncol), F32)
        acc_ref[...] = jnp.zeros((LANES, ncol), F32)

        def chunk(k, vt, valid=None):
            s = scores(k, valid)
            m_old = m_ref[...]
            m_new = jnp.maximum(m_old, jnp.max(s, axis=0, keepdims=True))
            alpha = jnp.exp2(m_old - m_new)
            p = jnp.exp2(s - m_new)
            l_ref[...] = alpha * l_ref[...] + jnp.sum(p, axis=0, keepdims=True)
            acc_ref[...] = alpha * acc_ref[...] + jnp.dot(vt, p.astype(BF16), preferred_element_type=F32)
            m_ref[...] = m_new

        chunk(kctx_ref[...], vctx_ref[0])

        @pl.when(is_latent)
        def _():
            if window:
                for k, vt, valid in window_blocks():
                    chunk(k, vt, valid)
            else:
                def body(c, carry):
                    start = pl.multiple_of(c * KC, KC)
                    chunk(klat_ref[pl.ds(start, KC), :], vlat_ref[0, :, pl.ds(start, KC)])
                    return carry
                lax.fori_loop(0, lat_len // KC, body, 0)

        finalize(acc_ref[...] * (1.0 / l_ref[...]))

    pl.when(use_bound)(bounded_pass)
    pl.when(jnp.logical_not(use_bound))(online_pass)


def _score_bound(q_gain, k_gain, group, sink=None):
    bound = jnp.max(jnp.abs(q_gain)) * jnp.max(jnp.abs(k_gain)) * (math.sqrt(group) * LOG2E * BOUND_MARGIN)
    if sink is not None:
        bound = jnp.maximum(bound, jnp.max(sink) * LOG2E)
    return bound.reshape(1).astype(F32)


def _attention(bound, q, k_all, vt_all, *, n_batch, s_len, ctx_len, ctx_queries, heads, n_groups, dual, window,
               lambda_init=0.0, lam_params=None, subln=None, sink=None):
    assert ctx_len == TQ
    width = heads * LANES
    nq = s_len // TQ
    ctx_blk0 = n_batch * s_len // ctx_len
    ncol = (2 if dual else heads) * TQ
    qmap = lambda b, g, i: (jnp.where(i < nq, b * nq + i, ctx_blk0 + b), g)

    in_specs = [pl.BlockSpec(memory_space=pltpu.SMEM),
                pl.BlockSpec((TQ, width), qmap),
                pl.BlockSpec((s_len, LANES), lambda b, g, i: (b, g)),
                pl.BlockSpec((1, LANES, s_len), lambda b, g, i: (g, 0, b)),
                pl.BlockSpec((ctx_len, LANES), lambda b, g, i: (ctx_blk0 + b, g)),
                pl.BlockSpec((1, LANES, ctx_len), lambda b, g, i: (g, 0, ctx_blk0 + b))]
    args = [bound, q, k_all, vt_all, k_all, vt_all]
    if dual:
        in_specs += [pl.BlockSpec((1, B_QK_DIM), lambda b, g, i: (0, 0))] * 4
        in_specs += [pl.BlockSpec((1, LANES), lambda b, g, i: (0, 0))]
        args += [p.reshape(1, B_QK_DIM) for p in lam_params] + [subln.reshape(1, LANES)]
    if sink is not None:
        in_specs += [pl.BlockSpec(memory_space=pltpu.SMEM)]
        args += [sink]

    kern = functools.partial(_attn_kernel, heads=heads, dual=dual, tq=TQ, lat_len=s_len, n_lat_steps=nq,
                             window=window, has_sink=sink is not None, lambda_init=lambda_init)
    return pl.pallas_call(
        kern,
        grid=(n_batch, n_groups, nq + (1 if ctx_queries else 0)),
        in_specs=in_specs,
        out_specs=pl.BlockSpec((TQ, width), qmap),
        out_shape=jax.ShapeDtypeStruct((q.shape[0], n_groups * width), BF16),
        scratch_shapes=[pltpu.VMEM((ncol, LANES), BF16), pltpu.VMEM((1, ncol), F32),
                        pltpu.VMEM((1, ncol), F32), pltpu.VMEM((LANES, ncol), F32)],
        compiler_params=_params(("arbitrary", "arbitrary", "arbitrary")),
        name="attention",
    )(*args)


def _merge_kernel(oa_ref, ob_ref, oc_ref, g_ref, wa32, wb32, wc32, m_ref, wa_ref, wb_ref, wc_ref):
    first = pl.program_id(0) == 0
    _cast_weight_once(wa32, wa_ref, first)
    _cast_weight_once(wb32, wb_ref, first)
    _cast_weight_once(wc32, wc_ref, first)
    oa, ob, oc = oa_ref[...], ob_ref[...], oc_ref[...]
    step = 4 * LANES
    for c in range(0, D_MODEL, step):
        sl = slice(c, c + step)
        m = g_ref[:, c:c + step].astype(F32) * jnp.dot(oa, wa_ref[:, sl], preferred_element_type=F32)
        m = m + g_ref[:, D_MODEL + c:D_MODEL + c + step].astype(F32) * jnp.dot(
            ob, wb_ref[:, sl], preferred_element_type=F32)
        m = m + g_ref[:, 2 * D_MODEL + c:2 * D_MODEL + c + step].astype(F32) * jnp.dot(
            oc, wc_ref[:, sl], preferred_element_type=F32)
        m_ref[:, sl] = m.astype(BF16)


def _merge(layer, oa, ob, oc, gates, wa, wb, wc, nt):
    const = lambda i: (layer, 0, 0)
    return pl.pallas_call(
        _merge_kernel,
        grid=(nt,),
        in_specs=[pl.BlockSpec((TM, A_Q_W), lambda i: (i, 0)),
                  pl.BlockSpec((TM, B_V_W), lambda i: (i, 0)),
                  pl.BlockSpec((TM, C_Q_W), lambda i: (i, 0)),
                  pl.BlockSpec((TM, 3 * D_MODEL), lambda i: (i, 0)),
                  _resident_weight_spec((1, A_Q_W, D_MODEL), const),
                  _resident_weight_spec((1, B_V_W, D_MODEL), const),
                  _resident_weight_spec((1, C_Q_W, D_MODEL), const)],
        out_specs=pl.BlockSpec((TM, D_MODEL), lambda i: (i, 0)),
        out_shape=jax.ShapeDtypeStruct((nt * TM, D_MODEL), BF16),
        scratch_shapes=[pltpu.VMEM((A_Q_W, D_MODEL), BF16), pltpu.VMEM((B_V_W, D_MODEL), BF16),
                        pltpu.VMEM((C_Q_W, D_MODEL), BF16)],
        compiler_params=_params(("arbitrary",)),
        name="merge",
    )(oa, ob, oc, gates, wa, wb, wc)


def _out_kernel(m_ref, w32_ref, x_ref, mod_ref, o_ref, w_ref):
    _cast_weight_once(w32_ref, w_ref, pl.program_id(0) == 0)
    m = m_ref[...]
    step = 4 * LANES
    for c in range(0, D_MODEL, step):
        o = jnp.dot(m, w_ref[:, c:c + step], preferred_element_type=F32)
        o_ref[:, c:c + step] = x_ref[:, c:c + step] + mod_ref[0, 2:3, c:c + step] * o


def _out_proj(layer, m, w_out, x_all, mod3, nt, tiles_per_batch, n_batch):
    return pl.pallas_call(
        _out_kernel,
        grid=(nt,),
        in_specs=[pl.BlockSpec((TM, D_MODEL), lambda i: (i, 0)),
                  _resident_weight_spec((1, D_MODEL, D_MODEL), lambda i: (layer, 0, 0)),
                  pl.BlockSpec((TM, D_MODEL), lambda i: (i, 0)),
                  pl.BlockSpec((1, 8, D_MODEL), lambda i: (_group_of_tile(i, tiles_per_batch, n_batch), 0, 0))],
        out_specs=pl.BlockSpec((TM, D_MODEL), lambda i: (i, 0)),
        out_shape=jax.ShapeDtypeStruct((nt * TM, D_MODEL), F32),
        scratch_shapes=[pltpu.VMEM((D_MODEL, D_MODEL), BF16)],
        compiler_params=_params(("arbitrary",)),
        name="out_proj",
    )(m, w_out, x_all, mod3)


def _norm_router_kernel(x_ref, g_ref, mod_ref, wr_ref, h_ref, aff_ref):
    h = _modulated_norm(x_ref[...], g_ref[...], mod_ref[0], 3, 4)
    h_ref[...] = h
    logits = _dot3(wr_ref[...], h, ((1,), (1,)))
    e = jnp.exp(logits - jnp.max(logits, axis=0, keepdims=True))
    aff_ref[...] = e / jnp.sum(e, axis=0, keepdims=True)


def _norm_router(x_all, gain, mod3, w_router_t, nt, tiles_per_batch, n_batch):
    d = x_all.shape[1]
    return pl.pallas_call(
        _norm_router_kernel,
        grid=(nt,),
        in_specs=[pl.BlockSpec((TM, d), lambda i: (i, 0)),
                  pl.BlockSpec((1, d), lambda i: (0, 0)),
                  pl.BlockSpec((1, 8, d), lambda i: (_group_of_tile(i, tiles_per_batch, n_batch), 0, 0)),
                  pl.BlockSpec((N_EXPERTS, d), lambda i: (0, 0))],
        out_specs=[pl.BlockSpec((TM, d), lambda i: (i, 0)),
                   pl.BlockSpec((N_EXPERTS, TM), lambda i: (0, i))],
        out_shape=[jax.ShapeDtypeStruct((nt * TM, d), F32),
                   jax.ShapeDtypeStruct((N_EXPERTS, nt * TM), F32)],
        compiler_params=_params(("arbitrary",)),
        name="norm_router",
    )(x_all, gain.reshape(1, d), mod3, w_router_t)


def _prefix_count(mask, n):
    ii = lax.broadcasted_iota(jnp.int32, (LANES, LANES), 0)
    jj = lax.broadcasted_iota(jnp.int32, (LANES, LANES), 1)
    tri = jnp.where(ii <= jj, 1.0, 0.0).astype(BF16)
    m = jnp.where(mask, 1.0, 0.0).astype(BF16)
    outs, starts = [], []
    off = jnp.zeros((mask.shape[0], 1), F32)
    for b in range(n // LANES):
        if (b * LANES) % TT == 0:
            starts.append(off)
        c = jnp.dot(m[:, b * LANES:(b + 1) * LANES], tri, preferred_element_type=F32) + off
        outs.append(c)
        off = c[:, LANES - 1:LANES]
    return jnp.concatenate(outs, axis=1), jnp.concatenate(starts, axis=1)


def _route_kernel(aff_ref, idx_ref, gate_ref, start_ref, csel_ref, *, n, cap, row_base):
    a = aff_ref[...]
    bits = pltpu.bitcast(a, jnp.int32)
    kf = float(cap)

    def search(i, t):
        cand = t | jnp.left_shift(jnp.int32(1), 30 - i)
        cnt = jnp.sum(jnp.where(bits >= cand, 1.0, 0.0), axis=1, keepdims=True)
        return jnp.where(cnt >= kf, cand, t)

    thr = lax.fori_loop(0, 31, search, jnp.zeros((N_EXPERTS, 1), jnp.int32))
    above = bits > thr
    equal = bits == thr
    need = kf - jnp.sum(jnp.where(above, 1.0, 0.0), axis=1, keepdims=True)
    chosen = jnp.logical_or(above, jnp.logical_and(equal, _prefix_count(equal, n)[0] <= need))
    slot_no, tile_starts = _prefix_count(chosen, n)
    csel_ref[...] = jnp.where(chosen, slot_no, 0.0)
    start_ref[0] = tile_starts.astype(jnp.int32)

    tok = lax.broadcasted_iota(jnp.int32, (1, n), 1).astype(F32)
    sub = min(cap, 64)
    base = row_base + pl.program_id(0) * n

    def per_expert(e, carry):
        ce = csel_ref[pl.ds(e, 1), :]
        ae = aff_ref[pl.ds(e, 1), :]
        for s0 in range(0, cap, sub):
            slot = (lax.broadcasted_iota(jnp.int32, (sub, n), 0) + (s0 + 1)).astype(F32)
            hit = ce == slot
            tsel = jnp.sum(jnp.where(hit, tok, 0.0), axis=1, keepdims=True)
            gsel = jnp.sum(jnp.where(hit, ae, 0.0), axis=1, keepdims=True)
            idx_ref[0, e, s0:s0 + sub, :] = tsel.astype(jnp.int32) + base
            gate_ref[0, e, s0:s0 + sub, :] = gsel
        return carry

    lax.fori_loop(0, N_EXPERTS, per_expert, 0)


def _route(aff_t, n_sets, n, cap, first_block, row_base):
    return pl.pallas_call(
        functools.partial(_route_kernel, n=n, cap=cap, row_base=row_base),
        grid=(n_sets,),
        in_specs=[pl.BlockSpec((N_EXPERTS, n), lambda s: (0, first_block + s))],
        out_specs=[pl.BlockSpec((1, N_EXPERTS, cap, 1), lambda s: (s, 0, 0, 0)),
                   pl.BlockSpec((1, N_EXPERTS, cap, 1), lambda s: (s, 0, 0, 0)),
                   pl.BlockSpec((1, N_EXPERTS, n // TT), lambda s: (s, 0, 0))],
        out_shape=[jax.ShapeDtypeStruct((n_sets, N_EXPERTS, cap, 1), jnp.int32),
                   jax.ShapeDtypeStruct((n_sets, N_EXPERTS, cap, 1), F32),
                   jax.ShapeDtypeStruct((n_sets, N_EXPERTS, n // TT), jnp.int32)],
        scratch_shapes=[pltpu.VMEM((N_EXPERTS, n), F32)],
        compiler_params=_params(("arbitrary",)),
        name="route",
    )(aff_t)


def _ffn_kernel(rows_ref, h_hbm, wg_ref, wu_ref, wd_ref, gate_ref, rowid_ref, mod_ref, y_ref,
                buf, xg, act, sem, *, nr, groups):
    e = pl.program_id(0)
    t = pl.program_id(1)
    d = h_hbm.shape[1]
    slot = e % 2
    per_step = nr // (N_UP + N_DOWN)

    def row_copy(expert, i, dst_slot):
        r = rows_ref[expert * nr + i]
        return pltpu.make_async_copy(h_hbm.at[pl.ds(r, 1), :], buf.at[dst_slot, pl.ds(i, 1), :], sem.at[dst_slot])

    def wait_rows(dst_slot):
        pltpu.make_async_copy(h_hbm.at[pl.ds(0, nr), :], buf.at[dst_slot], sem.at[dst_slot]).wait()

    def prefetch_next():
        nxt = jnp.minimum(e + 1, N_EXPERTS - 1)
        for i in range(per_step):
            row_copy(nxt, t * per_step + i, 1 - slot).start()

    @pl.when(jnp.logical_and(e == 0, t == 0))
    def _():
        def issue(i, carry):
            row_copy(0, i, 0).start()
            return carry
        lax.fori_loop(0, nr, issue, 0, unroll=8)

    @pl.when(t == 0)
    def _():
        wait_rows(slot)
        xg[...] = buf[slot].astype(BF16)

    @pl.when(t < N_UP)
    def _():
        prefetch_next()
        x = xg[...]
        u = jnp.dot(x, wg_ref[0, 0].astype(BF16), preferred_element_type=F32)
        v = jnp.dot(x, wu_ref[0, 0].astype(BF16), preferred_element_type=F32)
        hidden = (u * jax.nn.sigmoid(u) * v).astype(BF16)
        act[:, pl.ds(pl.multiple_of(t * FF_TILE, FF_TILE), FF_TILE)] = hidden

    @pl.when(t >= N_UP)
    def _():
        prefetch_next()
        y = jnp.dot(act[...], wd_ref[0, 0].astype(BF16), preferred_element_type=F32)
        cols = pl.ds(pl.multiple_of((t - N_UP) * OUT_TILE, OUT_TILE), OUT_TILE)
        for (r0, r1, grp) in groups:
            y_ref[0, r0:r1, cols] = (y[r0:r1, :] * gate_ref[0, r0:r1, :] * mod_ref[grp, 5:6, cols]).astype(BF16)

    @pl.when(t == N_UP + N_DOWN - 1)
    def _():
        rid = rowid_ref[0]
        lane = lax.broadcasted_iota(jnp.int32, (nr, LANES), 1)
        tag = jnp.where(lane == 0, rid // TT, jnp.where(lane == 1, rid % TT, 0))
        y_ref[0, 0:nr, d:d + LANES] = tag.astype(F32).astype(BF16)
        y_ref[0, nr:nr + WIN, :] = jnp.zeros((WIN, d + LANES), BF16)

    @pl.when(jnp.logical_and(e == N_EXPERTS - 1, t == N_UP + N_DOWN - 1))
    def _():
        wait_rows(1 - slot)


def _expert_ffn(layer, rows, h2, w_gate, w_up, w_down, gates, mod3, nr, groups):
    d = h2.shape[1]
    assert nr % (N_UP + N_DOWN) == 0 and EXPERT_FF == N_UP * FF_TILE and d == N_DOWN * OUT_TILE
    kern = functools.partial(_ffn_kernel, nr=nr, groups=groups)
    up_tile = lambda e, t, rows: (layer, e, 0, jnp.minimum(t, N_UP - 1))
    return pl.pallas_call(
        kern,
        grid_spec=pltpu.PrefetchScalarGridSpec(
            num_scalar_prefetch=1,
            grid=(N_EXPERTS, N_UP + N_DOWN),
            in_specs=[pl.BlockSpec(memory_space=pl.ANY),
                      pl.BlockSpec((1, 1, d, FF_TILE), up_tile),
                      pl.BlockSpec((1, 1, d, FF_TILE), up_tile),
                      pl.BlockSpec((1, 1, EXPERT_FF, OUT_TILE),
                                   lambda e, t, rows: (layer, e, 0, jnp.maximum(t - N_UP, 0))),
                      pl.BlockSpec((1, nr, 1), lambda e, t, rows: (e, 0, 0)),
                      pl.BlockSpec((1, nr, 1), lambda e, t, rows: (e, 0, 0)),
                      pl.BlockSpec((3, 8, d), lambda e, t, rows: (0, 0, 0))],
            out_specs=pl.BlockSpec((1, nr + WIN, d + LANES), lambda e, t, rows: (e, 0, 0)),
            scratch_shapes=[pltpu.VMEM((2, nr, d), F32), pltpu.VMEM((nr, d), BF16),
                            pltpu.VMEM((nr, EXPERT_FF), BF16), pltpu.SemaphoreType.DMA((2,))]),
        out_shape=jax.ShapeDtypeStruct((N_EXPERTS, nr + WIN, d + LANES), BF16),
        compiler_params=_params(("arbitrary", "arbitrary")),
        name="expert_ffn",
    )(rows.reshape(-1), h2, w_gate, w_up, w_down, gates, rows.reshape(N_EXPERTS, nr, 1), mod3)


def _combine_kernel(starts_ref, rounds_ref, y_hbm, x_ref, o_ref, stage, extra, sem, sem_x, *, n_tiles, nr):
    j = pl.program_id(0)
    slot = j % 2
    d = x_ref.shape[1]

    def windows(tile, rnd, dst, dsem):
        copies = []
        for e in range(N_EXPERTS):
            first = starts_ref[tile * N_EXPERTS + e]
            row = jnp.minimum(first - first % ROW_ALIGN + rnd * WIN, nr)
            copies.append(pltpu.make_async_copy(y_hbm.at[e, pl.ds(pl.multiple_of(row, ROW_ALIGN), WIN), :],
                                                dst.at[pl.ds(e * WIN, WIN), :], dsem))
        return copies

    @pl.when(j == 0)
    def _():
        for cp in windows(0, 0, stage.at[0], sem.at[0]):
            cp.start()

    @pl.when(j + 1 < n_tiles)
    def _():
        for cp in windows(j + 1, 0, stage.at[1 - slot], sem.at[1 - slot]):
            cp.start()

    for cp in windows(j, 0, stage.at[slot], sem.at[slot]):
        cp.wait()

    tok = (lax.broadcasted_iota(jnp.int32, (1, TT), 1) + j * TT).astype(F32)

    def placed(rows):
        ids = rows[:, d:d + 1].astype(F32) * float(TT) + rows[:, d + 1:d + 2].astype(F32)
        onehot = jnp.where(ids == tok, 1.0, 0.0).astype(BF16)
        return lax.dot_general(onehot, rows[:, 0:d], (((0,), (0,)), ((), ())), preferred_element_type=F32)

    o_ref[...] = x_ref[...] + placed(stage[slot])

    def more(rnd, carry):
        cps = windows(j, rnd, extra, sem_x)
        for cp in cps:
            cp.start()
        for cp in cps:
            cp.wait()
        o_ref[...] += placed(extra[...])
        return carry

    lax.fori_loop(1, rounds_ref[j], more, 0)


def _combine(starts, rounds, y, x_mid, n_tiles, nr):
    d = x_mid.shape[1]
    return pl.pallas_call(
        functools.partial(_combine_kernel, n_tiles=n_tiles, nr=nr),
        grid_spec=pltpu.PrefetchScalarGridSpec(
            num_scalar_prefetch=2,
            grid=(n_tiles,),
            in_specs=[pl.BlockSpec(memory_space=pl.ANY),
                      pl.BlockSpec((TT, d), lambda j, s, r: (j, 0))],
            out_specs=pl.BlockSpec((TT, d), lambda j, s, r: (j, 0)),
            scratch_shapes=[pltpu.VMEM((2, N_EXPERTS * WIN, d + LANES), BF16),
                            pltpu.VMEM((N_EXPERTS * WIN, d + LANES), BF16),
                            pltpu.SemaphoreType.DMA((2,)), pltpu.SemaphoreType.DMA(())]),
        out_shape=jax.ShapeDtypeStruct((n_tiles * TT, d), F32),
        compiler_params=_params(("arbitrary",)),
        name="combine",
    )(starts, rounds, y, x_mid)


def kernel(x, c, ctx, c_ctx, w_mod, b_mod, norm_mix, norm_ffn, w_in, qn_a, kn_a, qn_b, kn_b,
           lam_q1, lam_k1, lam_q2, lam_k2, subln_b, qn_c, kn_c, sink_c,
           w_br_a, w_br_b, w_br_c, w_out, w_router, w_gate, w_up, w_down):
    n_batch, s_len, d = x.shape
    ctx_len = ctx.shape[1]
    depth = w_mod.shape[0]
    lat_rows = n_batch * s_len
    tiles_per_batch = s_len // TM
    n_lat_tiles = lat_rows // TM
    assert s_len % TM == 0 and (n_batch * ctx_len) == TM and s_len % KC == 0 and d == D_MODEL
    assert KV_W == Q_W == D_MODEL and ctx_len % TT == 0 and TM % TT == 0

    cvec = jnp.zeros((8, d), F32).at[:n_batch].set(c).at[n_batch].set(c_ctx)
    mod_all = _modulation(cvec, w_mod, b_mod)

    tabs = _rope_tables(s_len, HEAD_DIM, TM) + _rope_tables(s_len, B_QK_DIM, TM)
    x_all = jnp.concatenate([x.reshape(lat_rows, d), ctx.reshape(n_batch * ctx_len, d)], axis=0)

    cap_lat = EC_CAPACITY * s_len // N_EXPERTS
    cap_ctx = EC_CAPACITY * ctx_len // N_EXPERTS
    lane_vec = lambda v: v.reshape(1, -1) if v.shape[-1] == LANES else jnp.tile(v, LANES // v.shape[-1]).reshape(1, LANES)

    for l in range(depth):
        last = l == depth - 1
        lambda_init = 0.8 - 0.6 * math.exp(-0.3 * l)
        nt = n_lat_tiles if last else n_lat_tiles + 1
        mod3 = jnp.pad(mod_all[l, :n_batch + 1].reshape(n_batch + 1, 6, d), ((0, 0), (0, 2), (0, 0)))

        h = _norm_mix(x_all, norm_mix[l], mod3, tiles_per_batch, n_batch)
        ka, kb, kc, vta, vtb, vtc = _kv_proj(l, h, w_in, tabs, lane_vec(kn_a[l]), lane_vec(kn_b[l]),
                                             lane_vec(kn_c[l]), tiles_per_batch, n_lat_tiles)
        qa, qb, qc = _q_proj(l, h, w_in, tabs, lane_vec(qn_a[l]), lane_vec(qn_b[l]),
                             lane_vec(qn_c[l]), tiles_per_batch, n_lat_tiles, nt)
        gates = _gate_proj(l, h, w_in, nt)

        common = dict(n_batch=n_batch, s_len=s_len, ctx_len=ctx_len, ctx_queries=not last)
        lam_params = (lam_q1[l], lam_k1[l], lam_q2[l], lam_k2[l])
        oa = _attention(_score_bound(qn_a[l], kn_a[l], HEAD_DIM), qa, ka, vta, heads=A_Q_HEADS // A_KV_HEADS,
                        n_groups=A_KV_HEADS, dual=False, window=False, **common)
        ob = _attention(_score_bound(qn_b[l], kn_b[l], B_QK_DIM), qb, kb, vtb, heads=1, n_groups=B_HEADS, dual=True,
                        window=False, lambda_init=lambda_init, lam_params=lam_params, subln=subln_b[l], **common)
        oc = _attention(_score_bound(qn_c[l], kn_c[l], HEAD_DIM, sink_c[l]), qc, kc, vtc,
                        heads=C_Q_HEADS // C_KV_HEADS, n_groups=C_KV_HEADS, dual=False, window=True,
                        sink=sink_c[l], **common)

        m = _merge(l, oa, ob, oc, gates, w_br_a, w_br_b, w_br_c, nt)
        x_mid = _out_proj(l, m, w_out, x_all, mod3, nt, tiles_per_batch, n_batch)

        h2, aff_t = _norm_router(x_mid, norm_ffn[l], mod3, w_router[l].T, nt, tiles_per_batch, n_batch)
        idx_lat, gate_lat, st_lat = _route(aff_t, n_batch, s_len, cap_lat, 0, 0)
        sets = [(idx_lat[b], gate_lat[b], st_lat[b], cap_lat, b) for b in range(n_batch)]
        if not last:
            idx_ctx, gate_ctx, st_ctx = _route(aff_t, n_batch, ctx_len, cap_ctx, lat_rows // ctx_len, lat_rows)
            sets += [(idx_ctx[b], gate_ctx[b], st_ctx[b], cap_ctx, n_batch) for b in range(n_batch)]
        idx_parts, gate_parts, groups, tile_lo, tile_hi, off = [], [], [], [], [], 0
        for idx, gate, st, cap, grp in sets:
            idx_parts.append(idx[:, :, 0])
            gate_parts.append(gate)
            groups.append((off, off + cap, grp))
            tile_lo.append(st.T + off)
            tile_hi.append(jnp.concatenate([st[:, 1:], jnp.full((N_EXPERTS, 1), cap, jnp.int32)], axis=1).T + off)
            off += cap
        rows = jnp.concatenate(idx_parts, axis=1)
        gate_rows = jnp.concatenate(gate_parts, axis=1)
        starts = jnp.concatenate(tile_lo, axis=0)
        counts = jnp.concatenate(tile_hi, axis=0) - starts
        rounds = jnp.max((starts % ROW_ALIGN + counts + (WIN - 1)) // WIN, axis=1)
        y = _expert_ffn(l, rows, h2, w_gate, w_up, w_down, gate_rows, mod3, off, tuple(groups))
        x_all = _combine(starts.reshape(-1), rounds, y, x_mid, nt * TM // TT, off)

    return x_all[:lat_rows].reshape(n_batch, s_len, d)
```

```python
import functools
import math

import jax
import jax.numpy as jnp
from jax import lax
from jax.experimental import pallas as pl
from jax.experimental.pallas import tpu as pltpu

F32 = jnp.float32
BF16 = jnp.bfloat16

D_MODEL = 2048
GRID_W = 64
HEAD_DIM = 128
A_Q_HEADS, A_KV_HEADS = 6, 2
B_HEADS, B_QK_DIM = 4, 64
C_Q_HEADS, C_KV_HEADS = 6, 2
WINDOW = 128
N_EXPERTS = 16
EXPERT_FF = 1024
EC_CAPACITY = 2
ROPE_THETA = 10000.0
EPS = 1e-6
NEG_INF = -1e30

A_Q_W = A_Q_HEADS * HEAD_DIM
A_KV_W = A_KV_HEADS * HEAD_DIM
B_QK_W = B_HEADS * 2 * B_QK_DIM
B_V_W = B_HEADS * HEAD_DIM
C_Q_W = C_Q_HEADS * HEAD_DIM
C_KV_W = C_KV_HEADS * HEAD_DIM
KV_W = 2 * A_KV_W + B_QK_W + B_V_W + 2 * C_KV_W
Q_W = A_Q_W + B_QK_W + C_Q_W

LANES = 128
TM = 512
TMO = 256
TQ_A, TQ_B, TQ_C = 256, 512, 256
KC = 512
FF_TILE = 256
N_UP = 4
OUT_TILE = 512
N_DOWN = 4
TT = 256
WIN = 64
ROW_ALIGN = 16
VMEM_LIMIT = 56 * 1024 * 1024
LOG2E = 1.4426950408889634
SAFE_LOG2_BOUND = 60.0
BOUND_MARGIN = 1.03


def _params(sem, vmem=VMEM_LIMIT):
    return pltpu.CompilerParams(dimension_semantics=sem, vmem_limit_bytes=vmem)


def _split_bf16(x):
    hi = x.astype(BF16)
    lo = (x - hi.astype(F32)).astype(BF16)
    return hi, lo


def _dot3(a, b, dims):
    ah, al = _split_bf16(a)
    bh, bl = _split_bf16(b)
    dg = lambda x, y: lax.dot_general(x, y, (dims, ((), ())), preferred_element_type=F32)
    return dg(ah, bh) + (dg(ah, bl) + dg(al, bh))


def _resident_weight_spec(block, index_map):
    return pl.BlockSpec(block, index_map, pipeline_mode=pl.Buffered(1))


def _cast_weight_once(w_ref, wbf_ref, first):
    @pl.when(first)
    def _():
        wbf_ref[...] = w_ref[0].astype(BF16)


def _mod_kernel(c_ref, w_ref, b_ref, o_ref):
    c = c_ref[...]
    a = c * jax.nn.sigmoid(c)
    o_ref[0] = _dot3(a, w_ref[0], ((1,), (0,))) + b_ref[0]


def _modulation(cvec, w_mod, b_mod):
    depth, d, n = w_mod.shape
    tn = 1024
    return pl.pallas_call(
        _mod_kernel,
        grid=(depth, n // tn),
        in_specs=[pl.BlockSpec((8, d), lambda l, j: (0, 0)),
                  pl.BlockSpec((1, d, tn), lambda l, j: (l, 0, j)),
                  pl.BlockSpec((1, 1, tn), lambda l, j: (l, 0, j))],
        out_specs=pl.BlockSpec((1, 8, tn), lambda l, j: (l, 0, j)),
        out_shape=jax.ShapeDtypeStruct((depth, 8, n), F32),
        compiler_params=_params(("arbitrary", "arbitrary")),
        name="modulation",
    )(cvec, w_mod, b_mod.reshape(depth, 1, n))


def _modulated_norm(x, gain, mod, shift_row, scale_row):
    ms = jnp.mean(x * x, axis=-1, keepdims=True)
    y = x * lax.rsqrt(ms + EPS) * gain
    return y * (1.0 + mod[scale_row:scale_row + 1, :]) + mod[shift_row:shift_row + 1, :]


def _norm_kernel(xl_ref, xc_ref, g_ref, mod_ref, h_ref, *, n_lat_tiles):
    is_latent = pl.program_id(0) < n_lat_tiles

    @pl.when(is_latent)
    def _():
        h_ref[...] = _modulated_norm(xl_ref[...], g_ref[...], mod_ref[0], 0, 1).astype(BF16)

    @pl.when(jnp.logical_not(is_latent))
    def _():
        h_ref[...] = _modulated_norm(xc_ref[...], g_ref[...], mod_ref[0], 0, 1).astype(BF16)


def _group_of_tile(i, tiles_per_batch, n_batch):
    return jnp.minimum(i // tiles_per_batch, n_batch)


def _norm_mix(x_lat, x_ctx, gain, mod3, tiles_per_batch, n_batch):
    d = x_lat.shape[1]
    n_lat_tiles = x_lat.shape[0] // TM
    r = x_lat.shape[0] + x_ctx.shape[0]
    return pl.pallas_call(
        functools.partial(_norm_kernel, n_lat_tiles=n_lat_tiles),
        grid=(r // TM,),
        in_specs=[pl.BlockSpec((TM, d), lambda i: (jnp.minimum(i, n_lat_tiles - 1), 0)),
                  pl.BlockSpec((TM, d), lambda i: (jnp.maximum(i - n_lat_tiles, 0), 0)),
                  pl.BlockSpec((1, d), lambda i: (0, 0)),
                  pl.BlockSpec((1, 8, d), lambda i: (_group_of_tile(i, tiles_per_batch, n_batch), 0, 0))],
        out_specs=pl.BlockSpec((TM, d), lambda i: (i, 0)),
        out_shape=jax.ShapeDtypeStruct((r, d), BF16),
        compiler_params=_params(("arbitrary",)),
        name="norm_mix",
    )(x_lat, x_ctx, gain.reshape(1, d), mod3)


def _rope_tables(s_len, dh, pad_rows):
    d_ax = dh // 2
    inv = ROPE_THETA ** (-jnp.arange(0, d_ax, 2, dtype=F32) / d_ax)
    t = jnp.arange(s_len, dtype=jnp.int32)
    fr = (t // GRID_W).astype(F32)[:, None] * inv
    fc = (t % GRID_W).astype(F32)[:, None] * inv
    ang = jnp.concatenate([fr, fr, fc, fc], axis=-1)
    cos, sin = jnp.cos(ang), jnp.sin(ang)
    quarter = jnp.arange(dh) // (dh // 4)
    sin_a = jnp.where(quarter % 2 == 0, -sin, 0.0)
    sin_b = jnp.where(quarter % 2 == 1, sin, 0.0)
    reps = LANES // dh

    def finish(tab, ident):
        tab = jnp.tile(tab, (1, reps))
        return jnp.concatenate([tab, jnp.full((pad_rows, LANES), ident, F32)], axis=0)

    return finish(cos, 1.0), finish(sin_a, 0.0), finish(sin_b, 0.0)


def _norm_rope(t, gain, cos, sin_a, sin_b, group, scale):
    sq = t * t
    if group == LANES:
        ms = jnp.mean(sq, axis=-1, keepdims=True)
    else:
        lane = lax.broadcasted_iota(jnp.int32, t.shape, 1)
        low = lane < group
        s_lo = jnp.sum(jnp.where(low, sq, 0.0), axis=-1, keepdims=True)
        s_hi = jnp.sum(jnp.where(low, 0.0, sq), axis=-1, keepdims=True)
        ms = jnp.where(low, s_lo, s_hi) * (1.0 / group)
    y = t * lax.rsqrt(ms + EPS) * gain
    q = group // 4
    r = y * cos + pltpu.roll(y, LANES - q, 1) * sin_a + pltpu.roll(y, q, 1) * sin_b
    if scale != 1.0:
        r = r * scale
    return r


def _kv_kernel(h_ref, w32_ref, c128, a128, b128, c64, a64, b64, gka, gkb, gkc,
               ka_ref, kb_ref, kc_ref, vta_ref, vtb_ref, vtc_ref, w_ref):
    _cast_weight_once(w32_ref, w_ref, pl.program_id(0) == 0)
    h = h_ref[...]

    def mm(c0):
        return jnp.dot(h, w_ref[:, c0:c0 + 2 * LANES], preferred_element_type=F32)

    def keys(c0, out_ref, o0, gain, cos, sa, sb, group):
        acc = mm(c0)
        for j in range(2):
            blk = acc[:, j * LANES:(j + 1) * LANES]
            out_ref[:, o0 + j * LANES:o0 + (j + 1) * LANES] = _norm_rope(
                blk, gain[...], cos[...], sa[...], sb[...], group, 1.0).astype(BF16)

    def values(c0, out_ref, h0):
        acc = mm(c0)
        for j in range(2):
            out_ref[h0 + j] = acc[:, j * LANES:(j + 1) * LANES].T.astype(BF16)

    c = 0
    keys(c, ka_ref, 0, gka, c128, a128, b128, HEAD_DIM)
    c += A_KV_W
    values(c, vta_ref, 0)
    c += A_KV_W
    for j in range(B_QK_W // (2 * LANES)):
        keys(c, kb_ref, j * 2 * LANES, gkb, c64, a64, b64, B_QK_DIM)
        c += 2 * LANES
    for j in range(B_V_W // (2 * LANES)):
        values(c, vtb_ref, 2 * j)
        c += 2 * LANES
    keys(c, kc_ref, 0, gkc, c128, a128, b128, HEAD_DIM)
    c += C_KV_W
    values(c, vtc_ref, 0)


def _table_specs(pos_tiles, n_lat_tiles):
    spec = pl.BlockSpec((TM, LANES), lambda i: (jnp.where(i < n_lat_tiles, i % pos_tiles, pos_tiles), 0))
    return [spec] * 6


def _kv_proj(layer, h, w_in, tabs, gka, gkb, gkc, pos_tiles, n_lat_tiles):
    r, d = h.shape
    nt = r // TM
    vec = pl.BlockSpec((1, LANES), lambda i: (0, 0))
    return pl.pallas_call(
        _kv_kernel,
        grid=(nt,),
        in_specs=[pl.BlockSpec((TM, d), lambda i: (i, 0)),
                  _resident_weight_spec((1, d, KV_W), lambda i: (layer, 0, 0))]
        + _table_specs(pos_tiles, n_lat_tiles) + [vec] * 3,
        out_specs=[pl.BlockSpec((TM, A_KV_W), lambda i: (i, 0)),
                   pl.BlockSpec((TM, B_QK_W), lambda i: (i, 0)),
                   pl.BlockSpec((TM, C_KV_W), lambda i: (i, 0)),
                   pl.BlockSpec((A_KV_HEADS, LANES, TM), lambda i: (0, 0, i)),
                   pl.BlockSpec((B_HEADS, LANES, TM), lambda i: (0, 0, i)),
                   pl.BlockSpec((C_KV_HEADS, LANES, TM), lambda i: (0, 0, i))],
        out_shape=[jax.ShapeDtypeStruct((r, A_KV_W), BF16),
                   jax.ShapeDtypeStruct((r, B_QK_W), BF16),
                   jax.ShapeDtypeStruct((r, C_KV_W), BF16),
                   jax.ShapeDtypeStruct((A_KV_HEADS, LANES, r), BF16),
                   jax.ShapeDtypeStruct((B_HEADS, LANES, r), BF16),
                   jax.ShapeDtypeStruct((C_KV_HEADS, LANES, r), BF16)],
        scratch_shapes=[pltpu.VMEM((d, KV_W), BF16)],
        compiler_params=_params(("arbitrary",)),
        name="kv_proj",
    )(h, w_in, *tabs, gka, gkb, gkc)


def _q_kernel(h_ref, w32_ref, c128, a128, b128, c64, a64, b64, gqa, gqb, gqc, qa_ref, qb_ref, qc_ref, w_ref):
    _cast_weight_once(w32_ref, w_ref, pl.program_id(0) == 0)
    h = h_ref[...]

    def queries(c0, out_ref, o0, gain, cos, sa, sb, group):
        acc = jnp.dot(h, w_ref[:, c0:c0 + 2 * LANES], preferred_element_type=F32)
        for j in range(2):
            blk = acc[:, j * LANES:(j + 1) * LANES]
            out_ref[:, o0 + j * LANES:o0 + (j + 1) * LANES] = _norm_rope(
                blk, gain[...], cos[...], sa[...], sb[...], group, group ** -0.5 * LOG2E).astype(BF16)

    c = 0
    for j in range(A_Q_W // (2 * LANES)):
        queries(c, qa_ref, j * 2 * LANES, gqa, c128, a128, b128, HEAD_DIM)
        c += 2 * LANES
    for j in range(B_QK_W // (2 * LANES)):
        queries(c, qb_ref, j * 2 * LANES, gqb, c64, a64, b64, B_QK_DIM)
        c += 2 * LANES
    for j in range(C_Q_W // (2 * LANES)):
        queries(c, qc_ref, j * 2 * LANES, gqc, c128, a128, b128, HEAD_DIM)
        c += 2 * LANES


def _q_proj(layer, h, w_in, tabs, gqa, gqb, gqc, pos_tiles, n_lat_tiles, nt):
    d = h.shape[1]
    rows = nt * TM
    vec = pl.BlockSpec((1, LANES), lambda i: (0, 0))
    return pl.pallas_call(
        _q_kernel,
        grid=(nt,),
        in_specs=[pl.BlockSpec((TM, d), lambda i: (i, 0)),
                  _resident_weight_spec((1, d, Q_W), lambda i: (layer, 0, KV_W // Q_W))]
        + _table_specs(pos_tiles, n_lat_tiles) + [vec] * 3,
        out_specs=[pl.BlockSpec((TM, A_Q_W), lambda i: (i, 0)),
                   pl.BlockSpec((TM, B_QK_W), lambda i: (i, 0)),
                   pl.BlockSpec((TM, C_Q_W), lambda i: (i, 0))],
        out_shape=[jax.ShapeDtypeStruct((rows, A_Q_W), BF16),
                   jax.ShapeDtypeStruct((rows, B_QK_W), BF16),
                   jax.ShapeDtypeStruct((rows, C_Q_W), BF16)],
        scratch_shapes=[pltpu.VMEM((d, Q_W), BF16)],
        compiler_params=_params(("arbitrary",)),
        name="q_proj",
    )(h, w_in, *tabs, gqa, gqb, gqc)


def _gate_kernel(h_ref, w32_ref, g_ref, w_ref):
    _cast_weight_once(w32_ref, w_ref, pl.program_id(1) == 0)
    h = h_ref[...]
    for c in range(0, D_MODEL, 2 * LANES):
        z = jnp.dot(h, w_ref[:, c:c + 2 * LANES], preferred_element_type=F32)
        g_ref[:, c:c + 2 * LANES] = jax.nn.sigmoid(z).astype(BF16)


def _gate_proj(layer, h, w_in, nt):
    d = h.shape[1]
    return pl.pallas_call(
        _gate_kernel,
        grid=(3, nt),
        in_specs=[pl.BlockSpec((TM, d), lambda j, i: (i, 0)),
                  _resident_weight_spec((1, d, D_MODEL), lambda j, i: (layer, 0, (KV_W + Q_W) // D_MODEL + j))],
        out_specs=pl.BlockSpec((TM, D_MODEL), lambda j, i: (i, j)),
        out_shape=jax.ShapeDtypeStruct((nt * TM, 3 * D_MODEL), BF16),
        scratch_shapes=[pltpu.VMEM((d, D_MODEL), BF16)],
        compiler_params=_params(("arbitrary", "arbitrary")),
        name="gate_proj",
    )(h, w_in)


def _attn_kernel(*refs, heads, dual, tq, tc, lat_len, window, has_sink, ctx_queries, lambda_init):
    it = iter(refs)
    bound_ref = next(it)
    q_ref = next(it)
    qc_ref = next(it) if ctx_queries else None
    klat_ref, vlat_ref, kctx_ref, vctx_ref = next(it), next(it), next(it), next(it)
    if dual:
        lq1, lk1, lq2, lk2, sub_ref = next(it), next(it), next(it), next(it), next(it)
    if has_sink:
        sink_ref = next(it)
    if window:
        band_ref = next(it)
    o_ref = next(it)
    oc_ref = next(it) if ctx_queries else None
    qe_ref, m_ref, l_ref, acc_ref = next(it), next(it), next(it), next(it)

    step = pl.program_id(2)
    bound = bound_ref[0]
    use_bound = bound <= SAFE_LOG2_BOUND

    def dense_blocks():
        for c in range(lat_len // KC):
            yield klat_ref[c * KC:(c + 1) * KC, :], vlat_ref[0, :, c * KC:(c + 1) * KC], None

    def window_blocks():
        nblk = lat_len // WINDOW
        for n, j in enumerate(range(-1, tq // WINDOW + 1)):
            blk = step * (tq // WINDOW) + j
            start = pl.multiple_of(jnp.clip(blk, 0, nblk - 1) * WINDOW, WINDOW)
            outside = jnp.where(jnp.logical_and(blk >= 0, blk < nblk), 0.0, NEG_INF)
            yield klat_ref[pl.ds(start, WINDOW), :], vlat_ref[0, :, pl.ds(start, WINDOW)], band_ref[n] + outside

    def attend(src_ref, dst_ref, t, latent):
        ncol = (2 if dual else heads) * t
        if dual:
            q = src_ref[...]
            lane = lax.broadcasted_iota(jnp.int32, q.shape, 1)
            zero = jnp.zeros_like(q)
            qe_ref[0:t, :] = jnp.where(lane < B_QK_DIM, q, zero)
            qe_ref[t:2 * t, :] = jnp.where(lane < B_QK_DIM, zero, q)
        else:
            for hh in range(heads):
                qe_ref[hh * t:(hh + 1) * t, :] = src_ref[:, hh * LANES:(hh + 1) * LANES]

        def scores(k, bias):
            s = lax.dot_general(k, qe_ref[0:ncol, :], (((1,), (1,)), ((), ())), preferred_element_type=F32)
            return s if bias is None else s + bias

        def blocks():
            if not latent:
                return ()
            return window_blocks() if window else dense_blocks()

        def sink_logits():
            g = pl.program_id(1)
            return jnp.concatenate(
                [jnp.full((1, t), sink_ref[g * heads + hh], F32) * LOG2E for hh in range(heads)], axis=1)

        def finalize(o):
            if dual:
                s1 = jnp.sum(lq1[...] * lk1[...], axis=-1, keepdims=True)
                s2 = jnp.sum(lq2[...] * lk2[...], axis=-1, keepdims=True)
                lam = jnp.exp(s1) - jnp.exp(s2) + lambda_init
                od = (o[:, 0:t] - lam * o[:, t:2 * t]).T
                ms = jnp.mean(od * od, axis=-1, keepdims=True)
                od = od * lax.rsqrt(ms + EPS) * sub_ref[...] * (1.0 - lambda_init)
                dst_ref[...] = od.astype(BF16)
            else:
                for hh in range(heads):
                    dst_ref[:, hh * LANES:(hh + 1) * LANES] = o[:, hh * t:(hh + 1) * t].T.astype(BF16)

        def bounded_pass():
            def weigh(s, vt):
                p = jnp.exp2(s - bound)
                return jnp.dot(vt, p.astype(BF16), preferred_element_type=F32), jnp.sum(p, axis=0, keepdims=True)

            acc = den = None
            pending = (scores(kctx_ref[...], None), vctx_ref[0])
            for k, vt, bias in blocks():
                s_next = scores(k, bias)
                da, dl = weigh(*pending)
                acc = da if acc is None else acc + da
                den = dl if den is None else den + dl
                pending = (s_next, vt)
            da, dl = weigh(*pending)
            acc = da if acc is None else acc + da
            den = dl if den is None else den + dl
            if has_sink:
                den = den + jnp.exp2(sink_logits() - bound)
            finalize(acc * (1.0 / den))

        def online_pass():
            if has_sink:
                m_ref[:, 0:ncol] = sink_logits()
                l_ref[:, 0:ncol] = jnp.ones((1, ncol), F32)
            else:
                m_ref[:, 0:ncol] = jnp.full((1, ncol), NEG_INF, F32)
                l_ref[:, 0:ncol] = jnp.zeros((1, ncol), F32)
            acc_ref[:, 0:ncol] = jnp.zeros((LANES, ncol), F32)

            def chunk(k, vt, bias=None):
                s = scores(k, bias)
                m_old = m_ref[:, 0:ncol]
                m_new = jnp.maximum(m_old, jnp.max(s, axis=0, keepdims=True))
                alpha = jnp.exp2(m_old - m_new)
                p = jnp.exp2(s - m_new)
                l_ref[:, 0:ncol] = alpha * l_ref[:, 0:ncol] + jnp.sum(p, axis=0, keepdims=True)
                acc_ref[:, 0:ncol] = alpha * acc_ref[:, 0:ncol] + jnp.dot(
                    vt, p.astype(BF16), preferred_element_type=F32)
                m_ref[:, 0:ncol] = m_new

            chunk(kctx_ref[...], vctx_ref[0])
            if latent and window:
                for k, vt, bias in window_blocks():
                    chunk(k, vt, bias)
            elif latent:
                def body(c, carry):
                    start = pl.multiple_of(c * KC, KC)
                    chunk(klat_ref[pl.ds(start, KC), :], vlat_ref[0, :, pl.ds(start, KC)])
                    return carry
                lax.fori_loop(0, lat_len // KC, body, 0)
            finalize(acc_ref[:, 0:ncol] * (1.0 / l_ref[:, 0:ncol]))

        pl.when(use_bound)(bounded_pass)
        pl.when(jnp.logical_not(use_bound))(online_pass)

    if ctx_queries:
        pl.when(step == 0)(lambda: attend(qc_ref, oc_ref, tc, False))
    attend(q_ref, o_ref, tq, True)


def _window_bias(heads, tq):
    n_blocks = tq // WINDOW + 2
    key = (jnp.arange(n_blocks)[:, None, None] - 1) * WINDOW + jnp.arange(WINDOW)[None, :, None]
    query = (jnp.arange(heads * tq) % tq)[None, None, :]
    return jnp.where(jnp.abs(key - query) <= WINDOW, 0.0, NEG_INF).astype(F32)


def _score_bound(q_gain, k_gain, group, sink=None):
    bound = jnp.max(jnp.abs(q_gain)) * jnp.max(jnp.abs(k_gain)) * (math.sqrt(group) * LOG2E * BOUND_MARGIN)
    if sink is not None:
        bound = jnp.maximum(bound, jnp.max(sink) * LOG2E)
    return bound.reshape(1).astype(F32)


def _attention(bound, q, k_all, vt_all, *, tq, n_batch, s_len, ctx_len, ctx_queries, heads, n_groups, dual, window,
               lambda_init=0.0, lam_params=None, subln=None, sink=None):
    width = heads * LANES
    nq = s_len // tq
    lat_rows = n_batch * s_len
    ctx_blk0 = lat_rows // ctx_len
    ncol = (2 if dual else heads) * tq
    cmap = lambda b, g, i: (ctx_blk0 + b, g)

    in_specs = [pl.BlockSpec(memory_space=pltpu.SMEM), pl.BlockSpec((tq, width), lambda b, g, i: (b * nq + i, g))]
    args = [bound, q]
    if ctx_queries:
        in_specs += [pl.BlockSpec((ctx_len, width), cmap)]
        args += [q]
    in_specs += [pl.BlockSpec((s_len, LANES), lambda b, g, i: (b, g)),
                 pl.BlockSpec((1, LANES, s_len), lambda b, g, i: (g, 0, b)),
                 pl.BlockSpec((ctx_len, LANES), cmap),
                 pl.BlockSpec((1, LANES, ctx_len), lambda b, g, i: (g, 0, ctx_blk0 + b))]
    args += [k_all, vt_all, k_all, vt_all]
    if dual:
        in_specs += [pl.BlockSpec((1, B_QK_DIM), lambda b, g, i: (0, 0))] * 4
        in_specs += [pl.BlockSpec((1, LANES), lambda b, g, i: (0, 0))]
        args += [p.reshape(1, B_QK_DIM) for p in lam_params] + [subln.reshape(1, LANES)]
    if sink is not None:
        in_specs += [pl.BlockSpec(memory_space=pltpu.SMEM)]
        args += [sink]
    if window:
        band = _window_bias(heads, tq)
        in_specs += [pl.BlockSpec(band.shape, lambda b, g, i: (0, 0, 0))]
        args += [band]

    out_specs = [pl.BlockSpec((tq, width), lambda b, g, i: (b * nq + i, g))]
    out_shape = [jax.ShapeDtypeStruct((lat_rows, n_groups * width), BF16)]
    if ctx_queries:
        out_specs += [pl.BlockSpec((ctx_len, width), lambda b, g, i: (b, g))]
        out_shape += [jax.ShapeDtypeStruct((n_batch * ctx_len, n_groups * width), BF16)]

    kern = functools.partial(_attn_kernel, heads=heads, dual=dual, tq=tq, tc=ctx_len, lat_len=s_len, window=window,
                             has_sink=sink is not None, ctx_queries=ctx_queries, lambda_init=lambda_init)
    outs = pl.pallas_call(
        kern,
        grid=(n_batch, n_groups, nq),
        in_specs=in_specs,
        out_specs=out_specs,
        out_shape=out_shape,
        scratch_shapes=[pltpu.VMEM((ncol, LANES), BF16), pltpu.VMEM((1, ncol), F32),
                        pltpu.VMEM((1, ncol), F32), pltpu.VMEM((LANES, ncol), F32)],
        compiler_params=_params(("arbitrary", "arbitrary", "arbitrary")),
        name="attention",
    )(*args)
    return (outs[0], outs[1]) if ctx_queries else (outs[0], None)


def _merge_kernel(oa_ref, ob_ref, oc_ref, ca_ref, cb_ref, cc_ref, g_ref, wa32, wb32, wc32, m_ref,
                  wa_ref, wb_ref, wc_ref, *, n_lat_tiles):
    i = pl.program_id(0)
    _cast_weight_once(wa32, wa_ref, i == 0)
    _cast_weight_once(wb32, wb_ref, i == 0)
    _cast_weight_once(wc32, wc_ref, i == 0)

    def merge(a_ref, b_ref, c_ref):
        oa, ob, oc = a_ref[...], b_ref[...], c_ref[...]
        step = 4 * LANES
        for c in range(0, D_MODEL, step):
            sl = slice(c, c + step)
            m = g_ref[:, c:c + step].astype(F32) * jnp.dot(oa, wa_ref[:, sl], preferred_element_type=F32)
            m = m + g_ref[:, D_MODEL + c:D_MODEL + c + step].astype(F32) * jnp.dot(
                ob, wb_ref[:, sl], preferred_element_type=F32)
            m = m + g_ref[:, 2 * D_MODEL + c:2 * D_MODEL + c + step].astype(F32) * jnp.dot(
                oc, wc_ref[:, sl], preferred_element_type=F32)
            m_ref[:, sl] = m.astype(BF16)

    pl.when(i < n_lat_tiles)(lambda: merge(oa_ref, ob_ref, oc_ref))
    pl.when(i >= n_lat_tiles)(lambda: merge(ca_ref, cb_ref, cc_ref))


def _merge(layer, lat_outs, ctx_outs, gates, wa, wb, wc, nt, n_lat_tiles):
    const = lambda i: (layer, 0, 0)
    if ctx_outs[0] is None:
        ctx_outs = lat_outs
    lat = lambda i: (jnp.minimum(i, n_lat_tiles - 1), 0)
    ctx = lambda i: (jnp.maximum(i - n_lat_tiles, 0), 0)
    return pl.pallas_call(
        functools.partial(_merge_kernel, n_lat_tiles=n_lat_tiles),
        grid=(nt,),
        in_specs=[pl.BlockSpec((TM, A_Q_W), lat),
                  pl.BlockSpec((TM, B_V_W), lat),
                  pl.BlockSpec((TM, C_Q_W), lat),
                  pl.BlockSpec((TM, A_Q_W), ctx),
                  pl.BlockSpec((TM, B_V_W), ctx),
                  pl.BlockSpec((TM, C_Q_W), ctx),
                  pl.BlockSpec((TM, 3 * D_MODEL), lambda i: (i, 0)),
                  _resident_weight_spec((1, A_Q_W, D_MODEL), const),
                  _resident_weight_spec((1, B_V_W, D_MODEL), const),
                  _resident_weight_spec((1, C_Q_W, D_MODEL), const)],
        out_specs=pl.BlockSpec((TM, D_MODEL), lambda i: (i, 0)),
        out_shape=jax.ShapeDtypeStruct((nt * TM, D_MODEL), BF16),
        scratch_shapes=[pltpu.VMEM((A_Q_W, D_MODEL), BF16), pltpu.VMEM((B_V_W, D_MODEL), BF16),
                        pltpu.VMEM((C_Q_W, D_MODEL), BF16)],
        compiler_params=_params(("arbitrary",)),
        name="merge",
    )(*lat_outs, *ctx_outs, gates, wa, wb, wc)


def _out_router_kernel(m_ref, w32_ref, xl_ref, xc_ref, mod_ref, g_ref, wr_ref, x_ref, h_ref, aff_ref, w_ref,
                       *, n_lat_tiles):
    i = pl.program_id(0)
    _cast_weight_once(w32_ref, w_ref, i == 0)
    m = m_ref[...]
    step = 4 * LANES

    def project(res_ref):
        for c in range(0, D_MODEL, step):
            o = jnp.dot(m, w_ref[:, c:c + step], preferred_element_type=F32)
            x_ref[:, c:c + step] = res_ref[:, c:c + step] + mod_ref[0, 2:3, c:c + step] * o

    pl.when(i < n_lat_tiles)(lambda: project(xl_ref))
    pl.when(i >= n_lat_tiles)(lambda: project(xc_ref))

    h = _modulated_norm(x_ref[...], g_ref[...], mod_ref[0], 3, 4)
    h_ref[...] = h
    logits = _dot3(wr_ref[...], h, ((1,), (1,)))
    e = jnp.exp(logits - jnp.max(logits, axis=0, keepdims=True))
    aff_ref[...] = e / jnp.sum(e, axis=0, keepdims=True)


def _out_router(layer, m, w_out, x_lat, x_ctx, ctx_tile0, mod3, gain, w_router_t, n_tiles, n_lat_tiles,
                tiles_per_batch, n_batch):
    d = D_MODEL
    return pl.pallas_call(
        functools.partial(_out_router_kernel, n_lat_tiles=n_lat_tiles),
        grid=(n_tiles,),
        in_specs=[pl.BlockSpec((TMO, d), lambda i: (i, 0)),
                  _resident_weight_spec((1, d, d), lambda i: (layer, 0, 0)),
                  pl.BlockSpec((TMO, d), lambda i: (jnp.minimum(i, n_lat_tiles - 1), 0)),
                  pl.BlockSpec((TMO, d), lambda i: (ctx_tile0 + jnp.maximum(i - n_lat_tiles, 0), 0)),
                  pl.BlockSpec((1, 8, d), lambda i: (_group_of_tile(i, tiles_per_batch, n_batch), 0, 0)),
                  pl.BlockSpec((1, d), lambda i: (0, 0)),
                  pl.BlockSpec((N_EXPERTS, d), lambda i: (0, 0))],
        out_specs=[pl.BlockSpec((TMO, d), lambda i: (i, 0)),
                   pl.BlockSpec((TMO, d), lambda i: (i, 0)),
                   pl.BlockSpec((N_EXPERTS, TMO), lambda i: (0, i))],
        out_shape=[jax.ShapeDtypeStruct((n_tiles * TMO, d), F32),
                   jax.ShapeDtypeStruct((n_tiles * TMO, d), F32),
                   jax.ShapeDtypeStruct((N_EXPERTS, n_tiles * TMO), F32)],
        scratch_shapes=[pltpu.VMEM((d, d), BF16)],
        compiler_params=_params(("arbitrary",)),
        name="out_router",
    )(m, w_out, x_lat, x_ctx, mod3, gain.reshape(1, d), w_router_t)


def _prefix_count(mask, n):
    ii = lax.broadcasted_iota(jnp.int32, (LANES, LANES), 0)
    jj = lax.broadcasted_iota(jnp.int32, (LANES, LANES), 1)
    tri = jnp.where(ii <= jj, 1.0, 0.0).astype(BF16)
    m = jnp.where(mask, 1.0, 0.0).astype(BF16)
    outs, starts = [], []
    off = jnp.zeros((mask.shape[0], 1), F32)
    for b in range(n // LANES):
        if (b * LANES) % TT == 0:
            starts.append(off)
        c = jnp.dot(m[:, b * LANES:(b + 1) * LANES], tri, preferred_element_type=F32) + off
        outs.append(c)
        off = c[:, LANES - 1:LANES]
    return jnp.concatenate(outs, axis=1), jnp.concatenate(starts, axis=1)


def _route_kernel(aff_ref, idx_ref, gate_ref, start_ref, csel_ref, *, n, cap, row_base):
    a = aff_ref[...]
    bits = pltpu.bitcast(a, jnp.int32)
    kf = float(cap)

    def search(i, t):
        cand = t | jnp.left_shift(jnp.int32(1), 30 - i)
        cnt = jnp.sum(jnp.where(bits >= cand, 1.0, 0.0), axis=1, keepdims=True)
        return jnp.where(cnt >= kf, cand, t)

    thr = lax.fori_loop(0, 31, search, jnp.zeros((N_EXPERTS, 1), jnp.int32))
    above = bits > thr
    equal = bits == thr
    need = kf - jnp.sum(jnp.where(above, 1.0, 0.0), axis=1, keepdims=True)
    chosen = jnp.logical_or(above, jnp.logical_and(equal, _prefix_count(equal, n)[0] <= need))
    slot_no, tile_starts = _prefix_count(chosen, n)
    csel_ref[...] = jnp.where(chosen, slot_no, 0.0)
    start_ref[0] = tile_starts.astype(jnp.int32)

    tok = lax.broadcasted_iota(jnp.int32, (1, n), 1).astype(F32)
    sub = min(cap, 64)
    base = row_base + pl.program_id(0) * n

    def per_expert(e, carry):
        ce = csel_ref[pl.ds(e, 1), :]
        ae = aff_ref[pl.ds(e, 1), :]
        for s0 in range(0, cap, sub):
            slot = (lax.broadcasted_iota(jnp.int32, (sub, n), 0) + (s0 + 1)).astype(F32)
            hit = ce == slot
            tsel = jnp.sum(jnp.where(hit, tok, 0.0), axis=1, keepdims=True)
            gsel = jnp.sum(jnp.where(hit, ae, 0.0), axis=1, keepdims=True)
            idx_ref[0, e, s0:s0 + sub, :] = tsel.astype(jnp.int32) + base
            gate_ref[0, e, s0:s0 + sub, :] = gsel
        return carry

    lax.fori_loop(0, N_EXPERTS, per_expert, 0)


def _route(aff_t, n_sets, n, cap, first_block, row_base):
    return pl.pallas_call(
        functools.partial(_route_kernel, n=n, cap=cap, row_base=row_base),
        grid=(n_sets,),
        in_specs=[pl.BlockSpec((N_EXPERTS, n), lambda s: (0, first_block + s))],
        out_specs=[pl.BlockSpec((1, N_EXPERTS, cap, 1), lambda s: (s, 0, 0, 0)),
                   pl.BlockSpec((1, N_EXPERTS, cap, 1), lambda s: (s, 0, 0, 0)),
                   pl.BlockSpec((1, N_EXPERTS, n // TT), lambda s: (s, 0, 0))],
        out_shape=[jax.ShapeDtypeStruct((n_sets, N_EXPERTS, cap, 1), jnp.int32),
                   jax.ShapeDtypeStruct((n_sets, N_EXPERTS, cap, 1), F32),
                   jax.ShapeDtypeStruct((n_sets, N_EXPERTS, n // TT), jnp.int32)],
        scratch_shapes=[pltpu.VMEM((N_EXPERTS, n), F32)],
        compiler_params=_params(("arbitrary",)),
        name="route",
    )(aff_t)


def _ffn_kernel(rows_ref, h_hbm, wg_ref, wu_ref, wd_ref, gate_ref, rowid_ref, mod_ref, y_ref,
                buf, xg, act, sem, *, nr, groups):
    e = pl.program_id(0)
    t = pl.program_id(1)
    d = h_hbm.shape[1]
    slot = e % 2
    per_step = nr // (N_UP + N_DOWN)

    def row_copy(expert, i, dst_slot):
        r = rows_ref[expert * nr + i]
        return pltpu.make_async_copy(h_hbm.at[pl.ds(r, 1), :], buf.at[dst_slot, pl.ds(i, 1), :], sem.at[dst_slot])

    def wait_rows(dst_slot):
        pltpu.make_async_copy(h_hbm.at[pl.ds(0, nr), :], buf.at[dst_slot], sem.at[dst_slot]).wait()

    def prefetch_next():
        nxt = jnp.minimum(e + 1, N_EXPERTS - 1)
        for i in range(per_step):
            row_copy(nxt, t * per_step + i, 1 - slot).start()

    @pl.when(jnp.logical_and(e == 0, t == 0))
    def _():
        def issue(i, carry):
            row_copy(0, i, 0).start()
            return carry
        lax.fori_loop(0, nr, issue, 0, unroll=8)

    @pl.when(t == 0)
    def _():
        wait_rows(slot)
        xg[...] = buf[slot].astype(BF16)

    @pl.when(t < N_UP)
    def _():
        prefetch_next()
        x = xg[...]
        u = jnp.dot(x, wg_ref[0, 0].astype(BF16), preferred_element_type=F32)
        v = jnp.dot(x, wu_ref[0, 0].astype(BF16), preferred_element_type=F32)
        hidden = (u * jax.nn.sigmoid(u) * v).astype(BF16)
        act[:, pl.ds(pl.multiple_of(t * FF_TILE, FF_TILE), FF_TILE)] = hidden

    @pl.when(t >= N_UP)
    def _():
        prefetch_next()
        y = jnp.dot(act[...], wd_ref[0, 0].astype(BF16), preferred_element_type=F32)
        cols = pl.ds(pl.multiple_of((t - N_UP) * OUT_TILE, OUT_TILE), OUT_TILE)
        for (r0, r1, grp) in groups:
            y_ref[0, r0:r1, cols] = (y[r0:r1, :] * gate_ref[0, r0:r1, :] * mod_ref[grp, 5:6, cols]).astype(BF16)

    @pl.when(t == N_UP + N_DOWN - 1)
    def _():
        rid = rowid_ref[0]
        lane = lax.broadcasted_iota(jnp.int32, (nr, LANES), 1)
        tag = jnp.where(lane == 0, rid // TT, jnp.where(lane == 1, rid % TT, 0))
        y_ref[0, 0:nr, d:d + LANES] = tag.astype(F32).astype(BF16)
        y_ref[0, nr:nr + WIN, :] = jnp.zeros((WIN, d + LANES), BF16)

    @pl.when(jnp.logical_and(e == N_EXPERTS - 1, t == N_UP + N_DOWN - 1))
    def _():
        wait_rows(1 - slot)


def _expert_ffn(layer, rows, h2, w_gate, w_up, w_down, gates, mod3, nr, groups):
    d = h2.shape[1]
    assert nr % (N_UP + N_DOWN) == 0 and EXPERT_FF == N_UP * FF_TILE and d == N_DOWN * OUT_TILE
    kern = functools.partial(_ffn_kernel, nr=nr, groups=groups)
    up_tile = lambda e, t, rows: (layer, e, 0, jnp.minimum(t, N_UP - 1))
    return pl.pallas_call(
        kern,
        grid_spec=pltpu.PrefetchScalarGridSpec(
            num_scalar_prefetch=1,
            grid=(N_EXPERTS, N_UP + N_DOWN),
            in_specs=[pl.BlockSpec(memory_space=pl.ANY),
                      pl.BlockSpec((1, 1, d, FF_TILE), up_tile),
                      pl.BlockSpec((1, 1, d, FF_TILE), up_tile),
                      pl.BlockSpec((1, 1, EXPERT_FF, OUT_TILE),
                                   lambda e, t, rows: (layer, e, 0, jnp.maximum(t - N_UP, 0))),
                      pl.BlockSpec((1, nr, 1), lambda e, t, rows: (e, 0, 0)),
                      pl.BlockSpec((1, nr, 1), lambda e, t, rows: (e, 0, 0)),
                      pl.BlockSpec((3, 8, d), lambda e, t, rows: (0, 0, 0))],
            out_specs=pl.BlockSpec((1, nr + WIN, d + LANES), lambda e, t, rows: (e, 0, 0)),
            scratch_shapes=[pltpu.VMEM((2, nr, d), F32), pltpu.VMEM((nr, d), BF16),
                            pltpu.VMEM((nr, EXPERT_FF), BF16), pltpu.SemaphoreType.DMA((2,))]),
        out_shape=jax.ShapeDtypeStruct((N_EXPERTS, nr + WIN, d + LANES), BF16),
        compiler_params=_params(("arbitrary", "arbitrary")),
        name="expert_ffn",
    )(rows.reshape(-1), h2, w_gate, w_up, w_down, gates, rows.reshape(N_EXPERTS, nr, 1), mod3)


def _combine_kernel(starts_ref, rounds_ref, y_hbm, x_ref, *rest, n_tiles, nr, emit_h):
    if emit_h:
        g_ref, mod_ref, o_ref, h_ref, stage, extra, sem, sem_x = rest
    else:
        o_ref, stage, extra, sem, sem_x = rest
    j = pl.program_id(0)
    slot = j % 2
    d = x_ref.shape[1]

    def windows(tile, rnd, dst, dsem):
        copies = []
        for e in range(N_EXPERTS):
            first = starts_ref[tile * N_EXPERTS + e]
            row = jnp.minimum(first - first % ROW_ALIGN + rnd * WIN, nr)
            copies.append(pltpu.make_async_copy(y_hbm.at[e, pl.ds(pl.multiple_of(row, ROW_ALIGN), WIN), :],
                                                dst.at[pl.ds(e * WIN, WIN), :], dsem))
        return copies

    @pl.when(j == 0)
    def _():
        for cp in windows(0, 0, stage.at[0], sem.at[0]):
            cp.start()

    @pl.when(j + 1 < n_tiles)
    def _():
        for cp in windows(j + 1, 0, stage.at[1 - slot], sem.at[1 - slot]):
            cp.start()

    for cp in windows(j, 0, stage.at[slot], sem.at[slot]):
        cp.wait()

    tok = (lax.broadcasted_iota(jnp.int32, (1, TT), 1) + j * TT).astype(F32)

    def placed(rows):
        ids = rows[:, d:d + 1].astype(F32) * float(TT) + rows[:, d + 1:d + 2].astype(F32)
        onehot = jnp.where(ids == tok, 1.0, 0.0).astype(BF16)
        return lax.dot_general(onehot, rows[:, 0:d], (((0,), (0,)), ((), ())), preferred_element_type=F32)

    o_ref[...] = x_ref[...] + placed(stage[slot])

    def more(rnd, carry):
        cps = windows(j, rnd, extra, sem_x)
        for cp in cps:
            cp.start()
        for cp in cps:
            cp.wait()
        o_ref[...] += placed(extra[...])
        return carry

    lax.fori_loop(1, rounds_ref[j], more, 0)
    if emit_h:
        h_ref[...] = _modulated_norm(o_ref[...], g_ref[...], mod_ref[0], 0, 1).astype(BF16)


def _combine(starts, rounds, y, x_mid, n_tiles, nr, next_norm=None):
    d = x_mid.shape[1]
    row_tile = pl.BlockSpec((TT, d), lambda j, s, r: (j, 0))
    in_specs = [pl.BlockSpec(memory_space=pl.ANY), row_tile]
    args = [starts, rounds, y, x_mid]
    out_specs, out_shape = row_tile, jax.ShapeDtypeStruct((n_tiles * TT, d), F32)
    if next_norm is not None:
        gain, mod3, tiles_per_batch, n_batch = next_norm
        in_specs += [pl.BlockSpec((1, d), lambda j, s, r: (0, 0)),
                     pl.BlockSpec((1, 8, d), lambda j, s, r: (_group_of_tile(j, tiles_per_batch, n_batch), 0, 0))]
        args += [gain.reshape(1, d), mod3]
        out_specs = [row_tile, row_tile]
        out_shape = [out_shape, jax.ShapeDtypeStruct((n_tiles * TT, d), BF16)]
    return pl.pallas_call(
        functools.partial(_combine_kernel, n_tiles=n_tiles, nr=nr, emit_h=next_norm is not None),
        grid_spec=pltpu.PrefetchScalarGridSpec(
            num_scalar_prefetch=2,
            grid=(n_tiles,),
            in_specs=in_specs,
            out_specs=out_specs,
            scratch_shapes=[pltpu.VMEM((2, N_EXPERTS * WIN, d + LANES), BF16),
                            pltpu.VMEM((N_EXPERTS * WIN, d + LANES), BF16),
                            pltpu.SemaphoreType.DMA((2,)), pltpu.SemaphoreType.DMA(())]),
        out_shape=out_shape,
        compiler_params=_params(("arbitrary",)),
        name="combine",
    )(*args)


def kernel(x, c, ctx, c_ctx, w_mod, b_mod, norm_mix, norm_ffn, w_in, qn_a, kn_a, qn_b, kn_b,
           lam_q1, lam_k1, lam_q2, lam_k2, subln_b, qn_c, kn_c, sink_c,
           w_br_a, w_br_b, w_br_c, w_out, w_router, w_gate, w_up, w_down):
    n_batch, s_len, d = x.shape
    ctx_len = ctx.shape[1]
    depth = w_mod.shape[0]
    lat_rows = n_batch * s_len
    tiles_per_batch = s_len // TM
    n_lat_tiles = lat_rows // TM
    assert s_len % TM == 0 and (n_batch * ctx_len) == TM and s_len % KC == 0 and d == D_MODEL
    assert KV_W == Q_W == D_MODEL and ctx_len % TT == 0 and TM % TT == 0

    cvec = jnp.zeros((8, d), F32).at[:n_batch].set(c).at[n_batch].set(c_ctx)
    mod_all = _modulation(cvec, w_mod, b_mod)

    tabs = _rope_tables(s_len, HEAD_DIM, TM) + _rope_tables(s_len, B_QK_DIM, TM)
    x_lat, x_ctx, ctx_tile0 = x.reshape(lat_rows, d), ctx.reshape(n_batch * ctx_len, d), 0
    mods = [jnp.pad(mod_all[l, :n_batch + 1].reshape(n_batch + 1, 6, d), ((0, 0), (0, 2), (0, 0)))
            for l in range(depth)]
    h = _norm_mix(x_lat, x_ctx, norm_mix[0], mods[0], tiles_per_batch, n_batch)

    cap_lat = EC_CAPACITY * s_len // N_EXPERTS
    cap_ctx = EC_CAPACITY * ctx_len // N_EXPERTS
    lane_vec = lambda v: v.reshape(1, -1) if v.shape[-1] == LANES else jnp.tile(v, LANES // v.shape[-1]).reshape(1, LANES)

    for l in range(depth):
        last = l == depth - 1
        lambda_init = 0.8 - 0.6 * math.exp(-0.3 * l)
        nt = n_lat_tiles if last else n_lat_tiles + 1
        mod3 = mods[l]
        ka, kb, kc, vta, vtb, vtc = _kv_proj(l, h, w_in, tabs, lane_vec(kn_a[l]), lane_vec(kn_b[l]),
                                             lane_vec(kn_c[l]), tiles_per_batch, n_lat_tiles)
        qa, qb, qc = _q_proj(l, h, w_in, tabs, lane_vec(qn_a[l]), lane_vec(qn_b[l]),
                             lane_vec(qn_c[l]), tiles_per_batch, n_lat_tiles, nt)
        gates = _gate_proj(l, h, w_in, nt)

        common = dict(n_batch=n_batch, s_len=s_len, ctx_len=ctx_len, ctx_queries=not last)
        lam_params = (lam_q1[l], lam_k1[l], lam_q2[l], lam_k2[l])
        oa, oa_c = _attention(_score_bound(qn_a[l], kn_a[l], HEAD_DIM), qa, ka, vta, tq=TQ_A,
                              heads=A_Q_HEADS // A_KV_HEADS, n_groups=A_KV_HEADS, dual=False, window=False, **common)
        ob, ob_c = _attention(_score_bound(qn_b[l], kn_b[l], B_QK_DIM), qb, kb, vtb, tq=TQ_B, heads=1,
                              n_groups=B_HEADS, dual=True, window=False, lambda_init=lambda_init,
                              lam_params=lam_params, subln=subln_b[l], **common)
        oc, oc_c = _attention(_score_bound(qn_c[l], kn_c[l], HEAD_DIM, sink_c[l]), qc, kc, vtc, tq=TQ_C,
                              heads=C_Q_HEADS // C_KV_HEADS, n_groups=C_KV_HEADS, dual=False, window=True,
                              sink=sink_c[l], **common)

        m = _merge(l, (oa, ob, oc), (oa_c, ob_c, oc_c), gates, w_br_a, w_br_b, w_br_c, nt, n_lat_tiles)
        x_mid, h2, aff_t = _out_router(l, m, w_out, x_lat, x_ctx, ctx_tile0, mod3, norm_ffn[l], w_router[l].T,
                                       nt * TM // TMO, lat_rows // TMO, s_len // TMO, n_batch)
        idx_lat, gate_lat, st_lat = _route(aff_t, n_batch, s_len, cap_lat, 0, 0)
        sets = [(idx_lat[b], gate_lat[b], st_lat[b], cap_lat, b) for b in range(n_batch)]
        if not last:
            idx_ctx, gate_ctx, st_ctx = _route(aff_t, n_batch, ctx_len, cap_ctx, lat_rows // ctx_len, lat_rows)
            sets += [(idx_ctx[b], gate_ctx[b], st_ctx[b], cap_ctx, n_batch) for b in range(n_batch)]
        idx_parts, gate_parts, groups, tile_lo, tile_hi, off = [], [], [], [], [], 0
        for idx, gate, st, cap, grp in sets:
            idx_parts.append(idx[:, :, 0])
            gate_parts.append(gate)
            groups.append((off, off + cap, grp))
            tile_lo.append(st.T + off)
            tile_hi.append(jnp.concatenate([st[:, 1:], jnp.full((N_EXPERTS, 1), cap, jnp.int32)], axis=1).T + off)
            off += cap
        rows = jnp.concatenate(idx_parts, axis=1)
        gate_rows = jnp.concatenate(gate_parts, axis=1)
        starts = jnp.concatenate(tile_lo, axis=0)
        counts = jnp.concatenate(tile_hi, axis=0) - starts
        rounds = jnp.max((starts % ROW_ALIGN + counts + (WIN - 1)) // WIN, axis=1)
        y = _expert_ffn(l, rows, h2, w_gate, w_up, w_down, gate_rows, mod3, off, tuple(groups))
        if last:
            x_lat = _combine(starts.reshape(-1), rounds, y, x_mid, nt * TM // TT, off)
        else:
            x_lat, h = _combine(starts.reshape(-1), rounds, y, x_mid, nt * TM // TT, off,
                                (norm_mix[l + 1], mods[l + 1], s_len // TT, n_batch))
            x_ctx, ctx_tile0 = x_lat, lat_rows // TMO

    return x_lat[:lat_rows].reshape(n_batch, s_len, d)
```

```python
import functools
import math

import jax
import jax.numpy as jnp
from jax import lax
from jax.experimental import pallas as pl
from jax.experimental.pallas import tpu as pltpu

F32 = jnp.float32
BF16 = jnp.bfloat16

D_MODEL = 2048
GRID_W = 64
HEAD_DIM = 128
A_Q_HEADS, A_KV_HEADS = 6, 2
B_HEADS, B_QK_DIM = 4, 64
C_Q_HEADS, C_KV_HEADS = 6, 2
WINDOW = 128
N_EXPERTS = 16
EXPERT_FF = 1024
EC_CAPACITY = 2
ROPE_THETA = 10000.0
EPS = 1e-6
NEG_INF = -1e30

A_Q_W = A_Q_HEADS * HEAD_DIM
A_KV_W = A_KV_HEADS * HEAD_DIM
B_QK_W = B_HEADS * 2 * B_QK_DIM
B_V_W = B_HEADS * HEAD_DIM
C_Q_W = C_Q_HEADS * HEAD_DIM
C_KV_W = C_KV_HEADS * HEAD_DIM
KV_W = 2 * A_KV_W + B_QK_W + B_V_W + 2 * C_KV_W
Q_W = A_Q_W + B_QK_W + C_Q_W

LANES = 128
TM = 512
TMO = 256
TQ_A, TQ_B, TQ_C = 512, 512, 512
KC = 512
FF_TILE = 256
N_UP = 4
OUT_TILE = 512
N_DOWN = 4
TT = 256
WIN = 64
ROW_ALIGN = 16
VMEM_LIMIT = 56 * 1024 * 1024
LOG2E = 1.4426950408889634
SAFE_LOG2_BOUND = 60.0
BOUND_MARGIN = 1.03


def _params(sem, vmem=VMEM_LIMIT):
    return pltpu.CompilerParams(dimension_semantics=sem, vmem_limit_bytes=vmem)


def _split_bf16(x):
    hi = x.astype(BF16)
    lo = (x - hi.astype(F32)).astype(BF16)
    return hi, lo


def _dot3(a, b, dims):
    ah, al = _split_bf16(a)
    bh, bl = _split_bf16(b)
    dg = lambda x, y: lax.dot_general(x, y, (dims, ((), ())), preferred_element_type=F32)
    return dg(ah, bh) + (dg(ah, bl) + dg(al, bh))


def _resident_weight_spec(block, index_map):
    return pl.BlockSpec(block, index_map, pipeline_mode=pl.Buffered(1))


def _cast_weight_once(w_ref, wbf_ref, first):
    @pl.when(first)
    def _():
        wbf_ref[...] = w_ref[0].astype(BF16)


def _mod_kernel(c_ref, w_ref, b_ref, o_ref):
    c = c_ref[...]
    a = c * jax.nn.sigmoid(c)
    o_ref[0] = _dot3(a, w_ref[0], ((1,), (0,))) + b_ref[0]


def _modulation(cvec, w_mod, b_mod):
    depth, d, n = w_mod.shape
    tn = 1024
    return pl.pallas_call(
        _mod_kernel,
        grid=(depth, n // tn),
        in_specs=[pl.BlockSpec((8, d), lambda l, j: (0, 0)),
                  pl.BlockSpec((1, d, tn), lambda l, j: (l, 0, j)),
                  pl.BlockSpec((1, 1, tn), lambda l, j: (l, 0, j))],
        out_specs=pl.BlockSpec((1, 8, tn), lambda l, j: (l, 0, j)),
        out_shape=jax.ShapeDtypeStruct((depth, 8, n), F32),
        compiler_params=_params(("arbitrary", "arbitrary")),
        name="modulation",
    )(cvec, w_mod, b_mod.reshape(depth, 1, n))


def _modulated_norm(x, gain, mod, shift_row, scale_row):
    ms = jnp.mean(x * x, axis=-1, keepdims=True)
    y = x * lax.rsqrt(ms + EPS) * gain
    return y * (1.0 + mod[scale_row:scale_row + 1, :]) + mod[shift_row:shift_row + 1, :]


def _norm_kernel(xl_ref, xc_ref, g_ref, mod_ref, h_ref, *, n_lat_tiles):
    is_latent = pl.program_id(0) < n_lat_tiles

    @pl.when(is_latent)
    def _():
        h_ref[...] = _modulated_norm(xl_ref[...], g_ref[...], mod_ref[0], 0, 1).astype(BF16)

    @pl.when(jnp.logical_not(is_latent))
    def _():
        h_ref[...] = _modulated_norm(xc_ref[...], g_ref[...], mod_ref[0], 0, 1).astype(BF16)


def _group_of_tile(i, tiles_per_batch, n_batch):
    return jnp.minimum(i // tiles_per_batch, n_batch)


def _norm_mix(x_lat, x_ctx, gain, mod3, tiles_per_batch, n_batch):
    d = x_lat.shape[1]
    n_lat_tiles = x_lat.shape[0] // TM
    r = x_lat.shape[0] + x_ctx.shape[0]
    return pl.pallas_call(
        functools.partial(_norm_kernel, n_lat_tiles=n_lat_tiles),
        grid=(r // TM,),
        in_specs=[pl.BlockSpec((TM, d), lambda i: (jnp.minimum(i, n_lat_tiles - 1), 0)),
                  pl.BlockSpec((TM, d), lambda i: (jnp.maximum(i - n_lat_tiles, 0), 0)),
                  pl.BlockSpec((1, d), lambda i: (0, 0)),
                  pl.BlockSpec((1, 8, d), lambda i: (_group_of_tile(i, tiles_per_batch, n_batch), 0, 0))],
        out_specs=pl.BlockSpec((TM, d), lambda i: (i, 0)),
        out_shape=jax.ShapeDtypeStruct((r, d), BF16),
        compiler_params=_params(("arbitrary",)),
        name="norm_mix",
    )(x_lat, x_ctx, gain.reshape(1, d), mod3)


def _rope_tables(s_len, dh, pad_rows):
    d_ax = dh // 2
    inv = ROPE_THETA ** (-jnp.arange(0, d_ax, 2, dtype=F32) / d_ax)
    t = jnp.arange(s_len, dtype=jnp.int32)
    fr = (t // GRID_W).astype(F32)[:, None] * inv
    fc = (t % GRID_W).astype(F32)[:, None] * inv
    ang = jnp.concatenate([fr, fr, fc, fc], axis=-1)
    cos, sin = jnp.cos(ang), jnp.sin(ang)
    quarter = jnp.arange(dh) // (dh // 4)
    sin_a = jnp.where(quarter % 2 == 0, -sin, 0.0)
    sin_b = jnp.where(quarter % 2 == 1, sin, 0.0)
    reps = LANES // dh

    def finish(tab, ident):
        tab = jnp.tile(tab, (1, reps))
        return jnp.concatenate([tab, jnp.full((pad_rows, LANES), ident, F32)], axis=0)

    return finish(cos, 1.0), finish(sin_a, 0.0), finish(sin_b, 0.0)


def _group_ones(group):
    ii = lax.broadcasted_iota(jnp.int32, (LANES, LANES), 0) // group
    jj = lax.broadcasted_iota(jnp.int32, (LANES, LANES), 1) // group
    return jnp.where(ii == jj, 1.0, 0.0).astype(BF16)


def _norm_rope(t, gain, cos, sin_a, sin_b, group, scale):
    ms = jnp.dot((t * t).astype(BF16), _group_ones(group), preferred_element_type=F32) * (1.0 / group)
    y = t * lax.rsqrt(ms + EPS) * gain
    q = group // 4
    r = y * cos + pltpu.roll(y, LANES - q, 1) * sin_a + pltpu.roll(y, q, 1) * sin_b
    if scale != 1.0:
        r = r * scale
    return r


def _kv_kernel(h_ref, w32_ref, c128, a128, b128, c64, a64, b64, gka, gkb, gkc,
               ka_ref, kb_ref, kc_ref, vta_ref, vtb_ref, vtc_ref, w_ref):
    _cast_weight_once(w32_ref, w_ref, pl.program_id(0) == 0)
    h = h_ref[...]

    def mm(c0):
        return jnp.dot(h, w_ref[:, c0:c0 + 2 * LANES], preferred_element_type=F32)

    def keys(acc, out_ref, o0, gain, cos, sa, sb, group):
        for j in range(2):
            blk = acc[:, j * LANES:(j + 1) * LANES]
            out_ref[:, o0 + j * LANES:o0 + (j + 1) * LANES] = _norm_rope(
                blk, gain[...], cos[...], sa[...], sb[...], group, 1.0).astype(BF16)

    def values(acc, out_ref, h0):
        for j in range(2):
            out_ref[h0 + j] = acc[:, j * LANES:(j + 1) * LANES].T.astype(BF16)

    jobs = [(keys, ka_ref, 0, gka, c128, a128, b128, HEAD_DIM), (values, vta_ref, 0)]
    jobs += [(keys, kb_ref, j * 2 * LANES, gkb, c64, a64, b64, B_QK_DIM) for j in range(B_QK_W // (2 * LANES))]
    jobs += [(values, vtb_ref, 2 * j) for j in range(B_V_W // (2 * LANES))]
    jobs += [(keys, kc_ref, 0, gkc, c128, a128, b128, HEAD_DIM), (values, vtc_ref, 0)]
    acc = mm(0)
    for n, (finish, *job) in enumerate(jobs):
        nxt = mm((n + 1) * 2 * LANES) if n + 1 < len(jobs) else None
        finish(acc, *job)
        acc = nxt


def _table_specs(pos_tiles, n_lat_tiles):
    spec = pl.BlockSpec((TM, LANES), lambda i: (jnp.where(i < n_lat_tiles, i % pos_tiles, pos_tiles), 0))
    return [spec] * 6


def _kv_proj(layer, h, w_in, tabs, gka, gkb, gkc, pos_tiles, n_lat_tiles):
    r, d = h.shape
    nt = r // TM
    vec = pl.BlockSpec((1, LANES), lambda i: (0, 0))
    return pl.pallas_call(
        _kv_kernel,
        grid=(nt,),
        in_specs=[pl.BlockSpec((TM, d), lambda i: (i, 0)),
                  _resident_weight_spec((1, d, KV_W), lambda i: (layer, 0, 0))]
        + _table_specs(pos_tiles, n_lat_tiles) + [vec] * 3,
        out_specs=[pl.BlockSpec((TM, A_KV_W), lambda i: (i, 0)),
                   pl.BlockSpec((TM, B_QK_W), lambda i: (i, 0)),
                   pl.BlockSpec((TM, C_KV_W), lambda i: (i, 0)),
                   pl.BlockSpec((A_KV_HEADS, LANES, TM), lambda i: (0, 0, i)),
                   pl.BlockSpec((B_HEADS, LANES, TM), lambda i: (0, 0, i)),
                   pl.BlockSpec((C_KV_HEADS, LANES, TM), lambda i: (0, 0, i))],
        out_shape=[jax.ShapeDtypeStruct((r, A_KV_W), BF16),
                   jax.ShapeDtypeStruct((r, B_QK_W), BF16),
                   jax.ShapeDtypeStruct((r, C_KV_W), BF16),
                   jax.ShapeDtypeStruct((A_KV_HEADS, LANES, r), BF16),
                   jax.ShapeDtypeStruct((B_HEADS, LANES, r), BF16),
                   jax.ShapeDtypeStruct((C_KV_HEADS, LANES, r), BF16)],
        scratch_shapes=[pltpu.VMEM((d, KV_W), BF16)],
        compiler_params=_params(("arbitrary",)),
        name="kv_proj",
    )(h, w_in, *tabs, gka, gkb, gkc)


def _q_kernel(h_ref, w32_ref, c128, a128, b128, c64, a64, b64, gqa, gqb, gqc, qa_ref, qb_ref, qc_ref, w_ref):
    _cast_weight_once(w32_ref, w_ref, pl.program_id(0) == 0)
    h = h_ref[...]

    def mm(c0):
        return jnp.dot(h, w_ref[:, c0:c0 + 2 * LANES], preferred_element_type=F32)

    def finish(acc, out_ref, o0, gain, cos, sa, sb, group):
        for j in range(2):
            blk = acc[:, j * LANES:(j + 1) * LANES]
            out_ref[:, o0 + j * LANES:o0 + (j + 1) * LANES] = _norm_rope(
                blk, gain[...], cos[...], sa[...], sb[...], group, group ** -0.5 * LOG2E).astype(BF16)

    jobs = []
    for j in range(A_Q_W // (2 * LANES)):
        jobs.append((qa_ref, j * 2 * LANES, gqa, c128, a128, b128, HEAD_DIM))
    for j in range(B_QK_W // (2 * LANES)):
        jobs.append((qb_ref, j * 2 * LANES, gqb, c64, a64, b64, B_QK_DIM))
    for j in range(C_Q_W // (2 * LANES)):
        jobs.append((qc_ref, j * 2 * LANES, gqc, c128, a128, b128, HEAD_DIM))
    acc = mm(0)
    for n, job in enumerate(jobs):
        nxt = mm((n + 1) * 2 * LANES) if n + 1 < len(jobs) else None
        finish(acc, *job)
        acc = nxt


def _q_proj(layer, h, w_in, tabs, gqa, gqb, gqc, pos_tiles, n_lat_tiles, nt):
    d = h.shape[1]
    rows = nt * TM
    vec = pl.BlockSpec((1, LANES), lambda i: (0, 0))
    return pl.pallas_call(
        _q_kernel,
        grid=(nt,),
        in_specs=[pl.BlockSpec((TM, d), lambda i: (i, 0)),
                  _resident_weight_spec((1, d, Q_W), lambda i: (layer, 0, KV_W // Q_W))]
        + _table_specs(pos_tiles, n_lat_tiles) + [vec] * 3,
        out_specs=[pl.BlockSpec((TM, A_Q_W), lambda i: (i, 0)),
                   pl.BlockSpec((TM, B_QK_W), lambda i: (i, 0)),
                   pl.BlockSpec((TM, C_Q_W), lambda i: (i, 0))],
        out_shape=[jax.ShapeDtypeStruct((rows, A_Q_W), BF16),
                   jax.ShapeDtypeStruct((rows, B_QK_W), BF16),
                   jax.ShapeDtypeStruct((rows, C_Q_W), BF16)],
        scratch_shapes=[pltpu.VMEM((d, Q_W), BF16)],
        compiler_params=_params(("arbitrary",)),
        name="q_proj",
    )(h, w_in, *tabs, gqa, gqb, gqc)


def _gate_kernel(h_ref, w32_ref, g_ref, w_ref):
    _cast_weight_once(w32_ref, w_ref, pl.program_id(1) == 0)
    h = h_ref[...]
    for c in range(0, D_MODEL, 2 * LANES):
        z = jnp.dot(h, w_ref[:, c:c + 2 * LANES], preferred_element_type=F32)
        g_ref[:, c:c + 2 * LANES] = jax.nn.sigmoid(z).astype(BF16)


def _gate_proj(layer, h, w_in, nt):
    d = h.shape[1]
    return pl.pallas_call(
        _gate_kernel,
        grid=(3, nt),
        in_specs=[pl.BlockSpec((TM, d), lambda j, i: (i, 0)),
                  _resident_weight_spec((1, d, D_MODEL), lambda j, i: (layer, 0, (KV_W + Q_W) // D_MODEL + j))],
        out_specs=pl.BlockSpec((TM, D_MODEL), lambda j, i: (i, j)),
        out_shape=jax.ShapeDtypeStruct((nt * TM, 3 * D_MODEL), BF16),
        scratch_shapes=[pltpu.VMEM((d, D_MODEL), BF16)],
        compiler_params=_params(("arbitrary", "arbitrary")),
        name="gate_proj",
    )(h, w_in)


def _attn_kernel(*refs, heads, dual, tq, tc, lat_len, window, has_sink, ctx_queries, lambda_init):
    it = iter(refs)
    bound_ref = next(it)
    q_ref = next(it)
    qc_ref = next(it) if ctx_queries else None
    klat_ref, vlat_ref, kctx_ref, vctx_ref = next(it), next(it), next(it), next(it)
    if dual:
        lq1, lk1, lq2, lk2, sub_ref = next(it), next(it), next(it), next(it), next(it)
    if has_sink:
        sink_ref = next(it)
    if window:
        band_ref = next(it)
    o_ref = next(it)
    oc_ref = next(it) if ctx_queries else None
    qe_ref, m_ref, l_ref, acc_ref = next(it), next(it), next(it), next(it)

    step = pl.program_id(2)
    bound = bound_ref[0]
    use_bound = bound <= SAFE_LOG2_BOUND

    def dense_blocks():
        for c in range(lat_len // KC):
            yield klat_ref[c * KC:(c + 1) * KC, :], vlat_ref[0, :, c * KC:(c + 1) * KC], None

    def window_blocks():
        nblk = lat_len // WINDOW
        for n, j in enumerate(range(-1, tq // WINDOW + 1)):
            blk = step * (tq // WINDOW) + j
            start = pl.multiple_of(jnp.clip(blk, 0, nblk - 1) * WINDOW, WINDOW)
            outside = jnp.where(jnp.logical_and(blk >= 0, blk < nblk), 0.0, NEG_INF)
            yield klat_ref[pl.ds(start, WINDOW), :], vlat_ref[0, :, pl.ds(start, WINDOW)], band_ref[n] + outside

    def attend(src_ref, dst_ref, t, latent):
        ncol = (2 if dual else heads) * t
        if dual:
            q = src_ref[...]
            lane = lax.broadcasted_iota(jnp.int32, q.shape, 1)
            zero = jnp.zeros_like(q)
            qe_ref[0:t, :] = jnp.where(lane < B_QK_DIM, q, zero)
            qe_ref[t:2 * t, :] = jnp.where(lane < B_QK_DIM, zero, q)
        else:
            for hh in range(heads):
                qe_ref[hh * t:(hh + 1) * t, :] = src_ref[:, hh * LANES:(hh + 1) * LANES]

        def scores(k, bias):
            s = lax.dot_general(k, qe_ref[0:ncol, :], (((1,), (1,)), ((), ())), preferred_element_type=F32)
            return s if bias is None else s + bias

        def blocks():
            if not latent:
                return ()
            return window_blocks() if window else dense_blocks()

        def sink_logits():
            g = pl.program_id(1)
            return jnp.concatenate(
                [jnp.full((1, t), sink_ref[g * heads + hh], F32) * LOG2E for hh in range(heads)], axis=1)

        def finalize(o):
            if dual:
                s1 = jnp.sum(lq1[...] * lk1[...], axis=-1, keepdims=True)
                s2 = jnp.sum(lq2[...] * lk2[...], axis=-1, keepdims=True)
                lam = jnp.exp(s1) - jnp.exp(s2) + lambda_init
                od = (o[:, 0:t] - lam * o[:, t:2 * t]).T
                ms = jnp.mean(od * od, axis=-1, keepdims=True)
                od = od * lax.rsqrt(ms + EPS) * sub_ref[...] * (1.0 - lambda_init)
                dst_ref[...] = od.astype(BF16)
            else:
                for hh in range(heads):
                    dst_ref[:, hh * LANES:(hh + 1) * LANES] = o[:, hh * t:(hh + 1) * t].T.astype(BF16)

        def bounded_pass():
            def weigh(s, vt):
                p = jnp.exp2(s - bound)
                return jnp.dot(vt, p.astype(BF16), preferred_element_type=F32), jnp.sum(p, axis=0, keepdims=True)

            acc = den = None
            pending = (scores(kctx_ref[...], None), vctx_ref[0])
            for k, vt, bias in blocks():
                s_next = scores(k, bias)
                da, dl = weigh(*pending)
                acc = da if acc is None else acc + da
                den = dl if den is None else den + dl
                pending = (s_next, vt)
            da, dl = weigh(*pending)
            acc = da if acc is None else acc + da
            den = dl if den is None else den + dl
            if has_sink:
                den = den + jnp.exp2(sink_logits() - bound)
            finalize(acc * (1.0 / den))

        def online_pass():
            if has_sink:
                m_ref[:, 0:ncol] = sink_logits()
                l_ref[:, 0:ncol] = jnp.ones((1, ncol), F32)
            else:
                m_ref[:, 0:ncol] = jnp.full((1, ncol), NEG_INF, F32)
                l_ref[:, 0:ncol] = jnp.zeros((1, ncol), F32)
            acc_ref[:, 0:ncol] = jnp.zeros((LANES, ncol), F32)

            def chunk(k, vt, bias=None):
                s = scores(k, bias)
                m_old = m_ref[:, 0:ncol]
                m_new = jnp.maximum(m_old, jnp.max(s, axis=0, keepdims=True))
                alpha = jnp.exp2(m_old - m_new)
                p = jnp.exp2(s - m_new)
                l_ref[:, 0:ncol] = alpha * l_ref[:, 0:ncol] + jnp.sum(p, axis=0, keepdims=True)
                acc_ref[:, 0:ncol] = alpha * acc_ref[:, 0:ncol] + jnp.dot(
                    vt, p.astype(BF16), preferred_element_type=F32)
                m_ref[:, 0:ncol] = m_new

            chunk(kctx_ref[...], vctx_ref[0])
            if latent and window:
                for k, vt, bias in window_blocks():
                    chunk(k, vt, bias)
            elif latent:
                def body(c, carry):
                    start = pl.multiple_of(c * KC, KC)
                    chunk(klat_ref[pl.ds(start, KC), :], vlat_ref[0, :, pl.ds(start, KC)])
                    return carry
                lax.fori_loop(0, lat_len // KC, body, 0)
            finalize(acc_ref[:, 0:ncol] * (1.0 / l_ref[:, 0:ncol]))

        pl.when(use_bound)(bounded_pass)
        pl.when(jnp.logical_not(use_bound))(online_pass)

    if ctx_queries:
        pl.when(step == 0)(lambda: attend(qc_ref, oc_ref, tc, False))
    attend(q_ref, o_ref, tq, True)


def _window_bias(heads, tq):
    n_blocks = tq // WINDOW + 2
    key = (jnp.arange(n_blocks)[:, None, None] - 1) * WINDOW + jnp.arange(WINDOW)[None, :, None]
    query = (jnp.arange(heads * tq) % tq)[None, None, :]
    return jnp.where(jnp.abs(key - query) <= WINDOW, 0.0, NEG_INF).astype(F32)


def _score_bound(q_gain, k_gain, group, sink=None):
    bound = jnp.max(jnp.abs(q_gain)) * jnp.max(jnp.abs(k_gain)) * (math.sqrt(group) * LOG2E * BOUND_MARGIN)
    if sink is not None:
        bound = jnp.maximum(bound, jnp.max(sink) * LOG2E)
    return bound.reshape(1).astype(F32)


def _attention(bound, q, k_all, vt_all, *, tq, n_batch, s_len, ctx_len, ctx_queries, heads, n_groups, dual, window,
               lambda_init=0.0, lam_params=None, subln=None, sink=None):
    width = heads * LANES
    nq = s_len // tq
    lat_rows = n_batch * s_len
    ctx_blk0 = lat_rows // ctx_len
    ncol = (2 if dual else heads) * tq
    cmap = lambda b, g, i: (ctx_blk0 + b, g)

    in_specs = [pl.BlockSpec(memory_space=pltpu.SMEM), pl.BlockSpec((tq, width), lambda b, g, i: (b * nq + i, g))]
    args = [bound, q]
    if ctx_queries:
        in_specs += [pl.BlockSpec((ctx_len, width), cmap)]
        args += [q]
    in_specs += [pl.BlockSpec((s_len, LANES), lambda b, g, i: (b, g)),
                 pl.BlockSpec((1, LANES, s_len), lambda b, g, i: (g, 0, b)),
                 pl.BlockSpec((ctx_len, LANES), cmap),
                 pl.BlockSpec((1, LANES, ctx_len), lambda b, g, i: (g, 0, ctx_blk0 + b))]
    args += [k_all, vt_all, k_all, vt_all]
    if dual:
        in_specs += [pl.BlockSpec((1, B_QK_DIM), lambda b, g, i: (0, 0))] * 4
        in_specs += [pl.BlockSpec((1, LANES), lambda b, g, i: (0, 0))]
        args += [p.reshape(1, B_QK_DIM) for p in lam_params] + [subln.reshape(1, LANES)]
    if sink is not None:
        in_specs += [pl.BlockSpec(memory_space=pltpu.SMEM)]
        args += [sink]
    if window:
        band = _window_bias(heads, tq)
        in_specs += [pl.BlockSpec(band.shape, lambda b, g, i: (0, 0, 0))]
        args += [band]

    out_specs = [pl.BlockSpec((tq, width), lambda b, g, i: (b * nq + i, g))]
    out_shape = [jax.ShapeDtypeStruct((lat_rows, n_groups * width), BF16)]
    if ctx_queries:
        out_specs += [pl.BlockSpec((ctx_len, width), lambda b, g, i: (b, g))]
        out_shape += [jax.ShapeDtypeStruct((n_batch * ctx_len, n_groups * width), BF16)]

    kern = functools.partial(_attn_kernel, heads=heads, dual=dual, tq=tq, tc=ctx_len, lat_len=s_len, window=window,
                             has_sink=sink is not None, ctx_queries=ctx_queries, lambda_init=lambda_init)
    outs = pl.pallas_call(
        kern,
        grid=(n_batch, n_groups, nq),
        in_specs=in_specs,
        out_specs=out_specs,
        out_shape=out_shape,
        scratch_shapes=[pltpu.VMEM((ncol, LANES), BF16), pltpu.VMEM((1, ncol), F32),
                        pltpu.VMEM((1, ncol), F32), pltpu.VMEM((LANES, ncol), F32)],
        compiler_params=_params(("arbitrary", "arbitrary", "arbitrary")),
        name="attention",
    )(*args)
    return (outs[0], outs[1]) if ctx_queries else (outs[0], None)


def _merge_kernel(oa_ref, ob_ref, oc_ref, ca_ref, cb_ref, cc_ref, g_ref, wa32, wb32, wc32, m_ref,
                  wa_ref, wb_ref, wc_ref, *, n_lat_tiles):
    i = pl.program_id(0)
    _cast_weight_once(wa32, wa_ref, i == 0)
    _cast_weight_once(wb32, wb_ref, i == 0)
    _cast_weight_once(wc32, wc_ref, i == 0)

    def merge(a_ref, b_ref, c_ref):
        oa, ob, oc = a_ref[...], b_ref[...], c_ref[...]
        step = 4 * LANES
        for c in range(0, D_MODEL, step):
            sl = slice(c, c + step)
            m = g_ref[:, c:c + step].astype(F32) * jnp.dot(oa, wa_ref[:, sl], preferred_element_type=F32)
            m = m + g_ref[:, D_MODEL + c:D_MODEL + c + step].astype(F32) * jnp.dot(
                ob, wb_ref[:, sl], preferred_element_type=F32)
            m = m + g_ref[:, 2 * D_MODEL + c:2 * D_MODEL + c + step].astype(F32) * jnp.dot(
                oc, wc_ref[:, sl], preferred_element_type=F32)
            m_ref[:, sl] = m.astype(BF16)

    pl.when(i < n_lat_tiles)(lambda: merge(oa_ref, ob_ref, oc_ref))
    pl.when(i >= n_lat_tiles)(lambda: merge(ca_ref, cb_ref, cc_ref))


def _merge(layer, lat_outs, ctx_outs, gates, wa, wb, wc, nt, n_lat_tiles):
    const = lambda i: (layer, 0, 0)
    if ctx_outs[0] is None:
        ctx_outs = lat_outs
    lat = lambda i: (jnp.minimum(i, n_lat_tiles - 1), 0)
    ctx = lambda i: (jnp.maximum(i - n_lat_tiles, 0), 0)
    return pl.pallas_call(
        functools.partial(_merge_kernel, n_lat_tiles=n_lat_tiles),
        grid=(nt,),
        in_specs=[pl.BlockSpec((TM, A_Q_W), lat),
                  pl.BlockSpec((TM, B_V_W), lat),
                  pl.BlockSpec((TM, C_Q_W), lat),
                  pl.BlockSpec((TM, A_Q_W), ctx),
                  pl.BlockSpec((TM, B_V_W), ctx),
                  pl.BlockSpec((TM, C_Q_W), ctx),
                  pl.BlockSpec((TM, 3 * D_MODEL), lambda i: (i, 0)),
                  _resident_weight_spec((1, A_Q_W, D_MODEL), const),
                  _resident_weight_spec((1, B_V_W, D_MODEL), const),
                  _resident_weight_spec((1, C_Q_W, D_MODEL), const)],
        out_specs=pl.BlockSpec((TM, D_MODEL), lambda i: (i, 0)),
        out_shape=jax.ShapeDtypeStruct((nt * TM, D_MODEL), BF16),
        scratch_shapes=[pltpu.VMEM((A_Q_W, D_MODEL), BF16), pltpu.VMEM((B_V_W, D_MODEL), BF16),
                        pltpu.VMEM((C_Q_W, D_MODEL), BF16)],
        compiler_params=_params(("arbitrary",)),
        name="merge",
    )(*lat_outs, *ctx_outs, gates, wa, wb, wc)


def _out_router_kernel(m_ref, w32_ref, xl_ref, xc_ref, mod_ref, g_ref, wr_ref, x_ref, h_ref, aff_ref, w_ref,
                       *, n_lat_tiles):
    i = pl.program_id(0)
    _cast_weight_once(w32_ref, w_ref, i == 0)
    m = m_ref[...]
    step = 4 * LANES

    def project(res_ref):
        for c in range(0, D_MODEL, step):
            o = jnp.dot(m, w_ref[:, c:c + step], preferred_element_type=F32)
            x_ref[:, c:c + step] = res_ref[:, c:c + step] + mod_ref[0, 2:3, c:c + step] * o

    pl.when(i < n_lat_tiles)(lambda: project(xl_ref))
    pl.when(i >= n_lat_tiles)(lambda: project(xc_ref))

    h = _modulated_norm(x_ref[...], g_ref[...], mod_ref[0], 3, 4)
    h_ref[...] = h
    logits = _dot3(wr_ref[...], h, ((1,), (1,)))
    e = jnp.exp(logits - jnp.max(logits, axis=0, keepdims=True))
    aff_ref[...] = e / jnp.sum(e, axis=0, keepdims=True)


def _out_router(layer, m, w_out, x_lat, x_ctx, ctx_tile0, mod3, gain, w_router_t, n_tiles, n_lat_tiles,
                tiles_per_batch, n_batch):
    d = D_MODEL
    return pl.pallas_call(
        functools.partial(_out_router_kernel, n_lat_tiles=n_lat_tiles),
        grid=(n_tiles,),
        in_specs=[pl.BlockSpec((TMO, d), lambda i: (i, 0)),
                  _resident_weight_spec((1, d, d), lambda i: (layer, 0, 0)),
                  pl.BlockSpec((TMO, d), lambda i: (jnp.minimum(i, n_lat_tiles - 1), 0)),
                  pl.BlockSpec((TMO, d), lambda i: (ctx_tile0 + jnp.maximum(i - n_lat_tiles, 0), 0)),
                  pl.BlockSpec((1, 8, d), lambda i: (_group_of_tile(i, tiles_per_batch, n_batch), 0, 0)),
                  pl.BlockSpec((1, d), lambda i: (0, 0)),
                  pl.BlockSpec((N_EXPERTS, d), lambda i: (0, 0))],
        out_specs=[pl.BlockSpec((TMO, d), lambda i: (i, 0)),
                   pl.BlockSpec((TMO, d), lambda i: (i, 0)),
                   pl.BlockSpec((N_EXPERTS, TMO), lambda i: (0, i))],
        out_shape=[jax.ShapeDtypeStruct((n_tiles * TMO, d), F32),
                   jax.ShapeDtypeStruct((n_tiles * TMO, d), F32),
                   jax.ShapeDtypeStruct((N_EXPERTS, n_tiles * TMO), F32)],
        scratch_shapes=[pltpu.VMEM((d, d), BF16)],
        compiler_params=_params(("arbitrary",)),
        name="out_router",
    )(m, w_out, x_lat, x_ctx, mod3, gain.reshape(1, d), w_router_t)


def _prefix_count(mask, n):
    ii = lax.broadcasted_iota(jnp.int32, (LANES, LANES), 0)
    jj = lax.broadcasted_iota(jnp.int32, (LANES, LANES), 1)
    tri = jnp.where(ii <= jj, 1.0, 0.0).astype(BF16)
    m = jnp.where(mask, 1.0, 0.0).astype(BF16)
    outs, starts = [], []
    off = jnp.zeros((mask.shape[0], 1), F32)
    for b in range(n // LANES):
        if (b * LANES) % TT == 0:
            starts.append(off)
        c = jnp.dot(m[:, b * LANES:(b + 1) * LANES], tri, preferred_element_type=F32) + off
        outs.append(c)
        off = c[:, LANES - 1:LANES]
    return jnp.concatenate(outs, axis=1), jnp.concatenate(starts, axis=1)


def _route_kernel(aff_ref, idx_ref, gate_ref, start_ref, csel_ref, *, n, cap, row_base):
    a = aff_ref[...]
    bits = pltpu.bitcast(a, jnp.int32)
    kf = float(cap)

    def search(i, t):
        cand = t | jnp.left_shift(jnp.int32(1), 30 - i)
        cnt = jnp.sum(jnp.where(bits >= cand, 1.0, 0.0), axis=1, keepdims=True)
        return jnp.where(cnt >= kf, cand, t)

    thr = lax.fori_loop(0, 31, search, jnp.zeros((N_EXPERTS, 1), jnp.int32))
    above = bits > thr
    equal = bits == thr
    need = kf - jnp.sum(jnp.where(above, 1.0, 0.0), axis=1, keepdims=True)
    chosen = jnp.logical_or(above, jnp.logical_and(equal, _prefix_count(equal, n)[0] <= need))
    slot_no, tile_starts = _prefix_count(chosen, n)
    csel_ref[...] = jnp.where(chosen, slot_no, 0.0)
    start_ref[0] = tile_starts.astype(jnp.int32)

    tok = lax.broadcasted_iota(jnp.int32, (1, n), 1).astype(F32)
    sub = min(cap, 64)
    base = row_base + pl.program_id(0) * n

    def per_expert(e, carry):
        ce = csel_ref[pl.ds(e, 1), :]
        ae = aff_ref[pl.ds(e, 1), :]
        for s0 in range(0, cap, sub):
            slot = (lax.broadcasted_iota(jnp.int32, (sub, n), 0) + (s0 + 1)).astype(F32)
            hit = ce == slot
            tsel = jnp.sum(jnp.where(hit, tok, 0.0), axis=1, keepdims=True)
            gsel = jnp.sum(jnp.where(hit, ae, 0.0), axis=1, keepdims=True)
            idx_ref[0, e, s0:s0 + sub, :] = tsel.astype(jnp.int32) + base
            gate_ref[0, e, s0:s0 + sub, :] = gsel
        return carry

    lax.fori_loop(0, N_EXPERTS, per_expert, 0)


def _route(aff_t, n_sets, n, cap, first_block, row_base):
    return pl.pallas_call(
        functools.partial(_route_kernel, n=n, cap=cap, row_base=row_base),
        grid=(n_sets,),
        in_specs=[pl.BlockSpec((N_EXPERTS, n), lambda s: (0, first_block + s))],
        out_specs=[pl.BlockSpec((1, N_EXPERTS, cap, 1), lambda s: (s, 0, 0, 0)),
                   pl.BlockSpec((1, N_EXPERTS, cap, 1), lambda s: (s, 0, 0, 0)),
                   pl.BlockSpec((1, N_EXPERTS, n // TT), lambda s: (s, 0, 0))],
        out_shape=[jax.ShapeDtypeStruct((n_sets, N_EXPERTS, cap, 1), jnp.int32),
                   jax.ShapeDtypeStruct((n_sets, N_EXPERTS, cap, 1), F32),
                   jax.ShapeDtypeStruct((n_sets, N_EXPERTS, n // TT), jnp.int32)],
        scratch_shapes=[pltpu.VMEM((N_EXPERTS, n), F32)],
        compiler_params=_params(("arbitrary",)),
        name="route",
    )(aff_t)


def _ffn_kernel(rows_ref, h_hbm, wg_ref, wu_ref, wd_ref, gate_ref, rowid_ref, mod_ref, y_ref,
                buf, xg, act, sem, *, nr, groups):
    e = pl.program_id(0)
    t = pl.program_id(1)
    d = h_hbm.shape[1]
    slot = e % 2
    per_step = nr // (N_UP + N_DOWN)

    def row_copy(expert, i, dst_slot):
        r = rows_ref[expert * nr + i]
        return pltpu.make_async_copy(h_hbm.at[pl.ds(r, 1), :], buf.at[dst_slot, pl.ds(i, 1), :], sem.at[dst_slot])

    def wait_rows(dst_slot):
        pltpu.make_async_copy(h_hbm.at[pl.ds(0, nr), :], buf.at[dst_slot], sem.at[dst_slot]).wait()

    def prefetch_next():
        nxt = jnp.minimum(e + 1, N_EXPERTS - 1)
        for i in range(per_step):
            row_copy(nxt, t * per_step + i, 1 - slot).start()

    @pl.when(jnp.logical_and(e == 0, t == 0))
    def _():
        def issue(i, carry):
            row_copy(0, i, 0).start()
            return carry
        lax.fori_loop(0, nr, issue, 0, unroll=8)

    @pl.when(t == 0)
    def _():
        wait_rows(slot)
        xg[...] = buf[slot].astype(BF16)

    @pl.when(t < N_UP)
    def _():
        prefetch_next()
        x = xg[...]
        u = jnp.dot(x, wg_ref[0, 0].astype(BF16), preferred_element_type=F32)
        v = jnp.dot(x, wu_ref[0, 0].astype(BF16), preferred_element_type=F32)
        hidden = (u * jax.nn.sigmoid(u) * v).astype(BF16)
        act[:, pl.ds(pl.multiple_of(t * FF_TILE, FF_TILE), FF_TILE)] = hidden

    @pl.when(t >= N_UP)
    def _():
        prefetch_next()
        y = jnp.dot(act[...], wd_ref[0, 0].astype(BF16), preferred_element_type=F32)
        cols = pl.ds(pl.multiple_of((t - N_UP) * OUT_TILE, OUT_TILE), OUT_TILE)
        for (r0, r1, grp) in groups:
            y_ref[0, r0:r1, cols] = (y[r0:r1, :] * gate_ref[0, r0:r1, :] * mod_ref[grp, 5:6, cols]).astype(BF16)

    @pl.when(t == N_UP + N_DOWN - 1)
    def _():
        rid = rowid_ref[0]
        lane = lax.broadcasted_iota(jnp.int32, (nr, LANES), 1)
        tag = jnp.where(lane == 0, rid // TT, jnp.where(lane == 1, rid % TT, 0))
        y_ref[0, 0:nr, d:d + LANES] = tag.astype(F32).astype(BF16)
        y_ref[0, nr:nr + WIN, :] = jnp.zeros((WIN, d + LANES), BF16)

    @pl.when(jnp.logical_and(e == N_EXPERTS - 1, t == N_UP + N_DOWN - 1))
    def _():
        wait_rows(1 - slot)


def _expert_ffn(layer, rows, h2, w_gate, w_up, w_down, gates, mod3, nr, groups):
    d = h2.shape[1]
    assert nr % (N_UP + N_DOWN) == 0 and EXPERT_FF == N_UP * FF_TILE and d == N_DOWN * OUT_TILE
    kern = functools.partial(_ffn_kernel, nr=nr, groups=groups)
    up_tile = lambda e, t, rows: (layer, e, 0, jnp.minimum(t, N_UP - 1))
    return pl.pallas_call(
        kern,
        grid_spec=pltpu.PrefetchScalarGridSpec(
            num_scalar_prefetch=1,
            grid=(N_EXPERTS, N_UP + N_DOWN),
            in_specs=[pl.BlockSpec(memory_space=pl.ANY),
                      pl.BlockSpec((1, 1, d, FF_TILE), up_tile),
                      pl.BlockSpec((1, 1, d, FF_TILE), up_tile),
                      pl.BlockSpec((1, 1, EXPERT_FF, OUT_TILE),
                                   lambda e, t, rows: (layer, e, 0, jnp.maximum(t - N_UP, 0))),
                      pl.BlockSpec((1, nr, 1), lambda e, t, rows: (e, 0, 0)),
                      pl.BlockSpec((1, nr, 1), lambda e, t, rows: (e, 0, 0)),
                      pl.BlockSpec((3, 8, d), lambda e, t, rows: (0, 0, 0))],
            out_specs=pl.BlockSpec((1, nr + WIN, d + LANES), lambda e, t, rows: (e, 0, 0)),
            scratch_shapes=[pltpu.VMEM((2, nr, d), F32), pltpu.VMEM((nr, d), BF16),
                            pltpu.VMEM((nr, EXPERT_FF), BF16), pltpu.SemaphoreType.DMA((2,))]),
        out_shape=jax.ShapeDtypeStruct((N_EXPERTS, nr + WIN, d + LANES), BF16),
        compiler_params=_params(("arbitrary", "arbitrary")),
        name="expert_ffn",
    )(rows.reshape(-1), h2, w_gate, w_up, w_down, gates, rows.reshape(N_EXPERTS, nr, 1), mod3)


def _combine_kernel(starts_ref, rounds_ref, y_hbm, x_ref, *rest, n_tiles, nr, emit_h):
    if emit_h:
        g_ref, mod_ref, o_ref, h_ref, stage, extra, sem, sem_x = rest
    else:
        o_ref, stage, extra, sem, sem_x = rest
    j = pl.program_id(0)
    slot = j % 2
    d = x_ref.shape[1]

    def windows(tile, rnd, dst, dsem):
        copies = []
        for e in range(N_EXPERTS):
            first = starts_ref[tile * N_EXPERTS + e]
            row = jnp.minimum(first - first % ROW_ALIGN + rnd * WIN, nr)
            copies.append(pltpu.make_async_copy(y_hbm.at[e, pl.ds(pl.multiple_of(row, ROW_ALIGN), WIN), :],
                                                dst.at[pl.ds(e * WIN, WIN), :], dsem))
        return copies

    @pl.when(j == 0)
    def _():
        for cp in windows(0, 0, stage.at[0], sem.at[0]):
            cp.start()

    @pl.when(j + 1 < n_tiles)
    def _():
        for cp in windows(j + 1, 0, stage.at[1 - slot], sem.at[1 - slot]):
            cp.start()

    for cp in windows(j, 0, stage.at[slot], sem.at[slot]):
        cp.wait()

    tok = (lax.broadcasted_iota(jnp.int32, (1, TT), 1) + j * TT).astype(F32)

    def placed(rows):
        ids = rows[:, d:d + 1].astype(F32) * float(TT) + rows[:, d + 1:d + 2].astype(F32)
        onehot = jnp.where(ids == tok, 1.0, 0.0).astype(BF16)
        return lax.dot_general(onehot, rows[:, 0:d], (((0,), (0,)), ((), ())), preferred_element_type=F32)

    o_ref[...] = x_ref[...] + placed(stage[slot])

    def more(rnd, carry):
        cps = windows(j, rnd, extra, sem_x)
        for cp in cps:
            cp.start()
        for cp in cps:
            cp.wait()
        o_ref[...] += placed(extra[...])
        return carry

    lax.fori_loop(1, rounds_ref[j], more, 0)
    if emit_h:
        h_ref[...] = _modulated_norm(o_ref[...], g_ref[...], mod_ref[0], 0, 1).astype(BF16)


def _combine(starts, rounds, y, x_mid, n_tiles, nr, next_norm=None):
    d = x_mid.shape[1]
    row_tile = pl.BlockSpec((TT, d), lambda j, s, r: (j, 0))
    in_specs = [pl.BlockSpec(memory_space=pl.ANY), row_tile]
    args = [starts, rounds, y, x_mid]
    out_specs, out_shape = row_tile, jax.ShapeDtypeStruct((n_tiles * TT, d), F32)
    if next_norm is not None:
        gain, mod3, tiles_per_batch, n_batch = next_norm
        in_specs += [pl.BlockSpec((1, d), lambda j, s, r: (0, 0)),
                     pl.BlockSpec((1, 8, d), lambda j, s, r: (_group_of_tile(j, tiles_per_batch, n_batch), 0, 0))]
        args += [gain.reshape(1, d), mod3]
        out_specs = [row_tile, row_tile]
        out_shape = [out_shape, jax.ShapeDtypeStruct((n_tiles * TT, d), BF16)]
    return pl.pallas_call(
        functools.partial(_combine_kernel, n_tiles=n_tiles, nr=nr, emit_h=next_norm is not None),
        grid_spec=pltpu.PrefetchScalarGridSpec(
            num_scalar_prefetch=2,
            grid=(n_tiles,),
            in_specs=in_specs,
            out_specs=out_specs,
            scratch_shapes=[pltpu.VMEM((2, N_EXPERTS * WIN, d + LANES), BF16),
                            pltpu.VMEM((N_EXPERTS * WIN, d + LANES), BF16),
                            pltpu.SemaphoreType.DMA((2,)), pltpu.SemaphoreType.DMA(())]),
        out_shape=out_shape,
        compiler_params=_params(("arbitrary",)),
        name="combine",
    )(*args)


def kernel(x, c, ctx, c_ctx, w_mod, b_mod, norm_mix, norm_ffn, w_in, qn_a, kn_a, qn_b, kn_b,
           lam_q1, lam_k1, lam_q2, lam_k2, subln_b, qn_c, kn_c, sink_c,
           w_br_a, w_br_b, w_br_c, w_out, w_router, w_gate, w_up, w_down):
    n_batch, s_len, d = x.shape
    ctx_len = ctx.shape[1]
    depth = w_mod.shape[0]
    lat_rows = n_batch * s_len
    tiles_per_batch = s_len // TM
    n_lat_tiles = lat_rows // TM
    assert s_len % TM == 0 and (n_batch * ctx_len) == TM and s_len % KC == 0 and d == D_MODEL
    assert KV_W == Q_W == D_MODEL and ctx_len % TT == 0 and TM % TT == 0

    cvec = jnp.zeros((8, d), F32).at[:n_batch].set(c).at[n_batch].set(c_ctx)
    mod_all = _modulation(cvec, w_mod, b_mod)

    tabs = _rope_tables(s_len, HEAD_DIM, TM) + _rope_tables(s_len, B_QK_DIM, TM)
    x_lat, x_ctx, ctx_tile0 = x.reshape(lat_rows, d), ctx.reshape(n_batch * ctx_len, d), 0
    mods = [jnp.pad(mod_all[l, :n_batch + 1].reshape(n_batch + 1, 6, d), ((0, 0), (0, 2), (0, 0)))
            for l in range(depth)]
    h = _norm_mix(x_lat, x_ctx, norm_mix[0], mods[0], tiles_per_batch, n_batch)

    cap_lat = EC_CAPACITY * s_len // N_EXPERTS
    cap_ctx = EC_CAPACITY * ctx_len // N_EXPERTS
    lane_vec = lambda v: v.reshape(1, -1) if v.shape[-1] == LANES else jnp.tile(v, LANES // v.shape[-1]).reshape(1, LANES)

    for l in range(depth):
        last = l == depth - 1
        lambda_init = 0.8 - 0.6 * math.exp(-0.3 * l)
        nt = n_lat_tiles if last else n_lat_tiles + 1
        mod3 = mods[l]
        ka, kb, kc, vta, vtb, vtc = _kv_proj(l, h, w_in, tabs, lane_vec(kn_a[l]), lane_vec(kn_b[l]),
                                             lane_vec(kn_c[l]), tiles_per_batch, n_lat_tiles)
        qa, qb, qc = _q_proj(l, h, w_in, tabs, lane_vec(qn_a[l]), lane_vec(qn_b[l]),
                             lane_vec(qn_c[l]), tiles_per_batch, n_lat_tiles, nt)
        gates = _gate_proj(l, h, w_in, nt)

        common = dict(n_batch=n_batch, s_len=s_len, ctx_len=ctx_len, ctx_queries=not last)
        lam_params = (lam_q1[l], lam_k1[l], lam_q2[l], lam_k2[l])
        oa, oa_c = _attention(_score_bound(qn_a[l], kn_a[l], HEAD_DIM), qa, ka, vta, tq=TQ_A,
                              heads=A_Q_HEADS // A_KV_HEADS, n_groups=A_KV_HEADS, dual=False, window=False, **common)
        ob, ob_c = _attention(_score_bound(qn_b[l], kn_b[l], B_QK_DIM), qb, kb, vtb, tq=TQ_B, heads=1,
                              n_groups=B_HEADS, dual=True, window=False, lambda_init=lambda_init,
                              lam_params=lam_params, subln=subln_b[l], **common)
        oc, oc_c = _attention(_score_bound(qn_c[l], kn_c[l], HEAD_DIM, sink_c[l]), qc, kc, vtc, tq=TQ_C,
                              heads=C_Q_HEADS // C_KV_HEADS, n_groups=C_KV_HEADS, dual=False, window=True,
                              sink=sink_c[l], **common)

        m = _merge(l, (oa, ob, oc), (oa_c, ob_c, oc_c), gates, w_br_a, w_br_b, w_br_c, nt, n_lat_tiles)
        x_mid, h2, aff_t = _out_router(l, m, w_out, x_lat, x_ctx, ctx_tile0, mod3, norm_ffn[l], w_router[l].T,
                                       nt * TM // TMO, lat_rows // TMO, s_len // TMO, n_batch)
        idx_lat, gate_lat, st_lat = _route(aff_t, n_batch, s_len, cap_lat, 0, 0)
        sets = [(idx_lat[b], gate_lat[b], st_lat[b], cap_lat, b) for b in range(n_batch)]
        if not last:
            idx_ctx, gate_ctx, st_ctx = _route(aff_t, n_batch, ctx_len, cap_ctx, lat_rows // ctx_len, lat_rows)
            sets += [(idx_ctx[b], gate_ctx[b], st_ctx[b], cap_ctx, n_batch) for b in range(n_batch)]
        idx_parts, gate_parts, groups, tile_lo, tile_hi, off = [], [], [], [], [], 0
        for idx, gate, st, cap, grp in sets:
            idx_parts.append(idx[:, :, 0])
            gate_parts.append(gate)
            groups.append((off, off + cap, grp))
            tile_lo.append(st.T + off)
            tile_hi.append(jnp.concatenate([st[:, 1:], jnp.full((N_EXPERTS, 1), cap, jnp.int32)], axis=1).T + off)
            off += cap
        rows = jnp.concatenate(idx_parts, axis=1)
        gate_rows = jnp.concatenate(gate_parts, axis=1)
        starts = jnp.concatenate(tile_lo, axis=0)
        counts = jnp.concatenate(tile_hi, axis=0) - starts
        rounds = jnp.max((starts % ROW_ALIGN + counts + (WIN - 1)) // WIN, axis=1)
        y = _expert_ffn(l, rows, h2, w_gate, w_up, w_down, gate_rows, mod3, off, tuple(groups))
        if last:
            x_lat = _combine(starts.reshape(-1), rounds, y, x_mid, nt * TM // TT, off)
        else:
            x_lat, h = _combine(starts.reshape(-1), rounds, y, x_mid, nt * TM // TT, off,
                                (norm_mix[l + 1], mods[l + 1], s_len // TT, n_batch))
            x_ctx, ctx_tile0 = x_lat, lat_rows // TMO

    return x_lat[:lat_rows].reshape(n_batch, s_len, d)
```
